```python
import jax, jax.numpy as jnp
from jax import lax
import numpy as np

D_MODEL = 1024
BATCH = 32
SEQ = 2048
DEPTH = 2

HEAD_DIM = 64
D_MIX = D_MODEL
GROUP_W = D_MIX // 4
GROUP_HEADS = GROUP_W // HEAD_DIM
N_MIX_HEADS = D_MIX // HEAD_DIM
NSA_KV_DIM = HEAD_DIM
ATTN_SCALE = HEAD_DIM ** -0.5
ROPE_THETA = 500000.0
ROT_DIM = HEAD_DIM // 4
HG_CHUNK = 64
HG_SUB = 16
Q_BLOCK = 128
NSA_CMP_LEN = 32
NSA_CMP_STRIDE = 16
NSA_CMP_HIDDEN = HEAD_DIM
NSA_SEL_BLOCK = 64
NSA_TOP_N = 16
NSA_WINDOW = 512
NSA_SEL_Q_BLOCK = 32
NSA_FORCE = 1.0e4
NEG_BIG = -1.0e30
EPS = 1e-6

IN_WIDTHS = (
    GROUP_W, GROUP_W, GROUP_W,
    GROUP_W, GROUP_W, GROUP_W, GROUP_HEADS,
    GROUP_W, GROUP_W, GROUP_W,
    GROUP_W,
    NSA_KV_DIM, NSA_KV_DIM,
    NSA_KV_DIM, NSA_KV_DIM,
    NSA_KV_DIM, NSA_KV_DIM,
    3 * GROUP_HEADS,
    D_MIX,
)
N_IN = sum(IN_WIDTHS)

kernel_name = "hymba_hgrn2_fox_stickbreak_nsa_trunk"


def _rmsnorm(x, g):
    xf = x.astype(jnp.float32)
    y = xf * lax.rsqrt(jnp.mean(xf * xf, axis=-1, keepdims=True) + EPS)
    return (y * g.astype(jnp.float32)).astype(x.dtype)


def _to_heads(a, n_heads):
    B, T, _ = a.shape
    return a.reshape(B, T, n_heads, -1).transpose(0, 2, 1, 3)


def _from_heads(a):
    B, H, T, d = a.shape
    return a.transpose(0, 2, 1, 3).reshape(B, T, H * d)


def _partial_rope(x, pos):
    half = ROT_DIM // 2
    inv_freq = ROPE_THETA ** (-(jnp.arange(half, dtype=jnp.float32) * 2.0 / ROT_DIM))
    ang = pos.astype(jnp.float32)[:, None] * inv_freq[None, :]
    cos, sin = jnp.cos(ang), jnp.sin(ang)
    xr = x[..., :ROT_DIM].astype(jnp.float32)
    x1, x2 = xr[..., :half], xr[..., half:]
    rot = jnp.concatenate([x1 * cos - x2 * sin, x2 * cos + x1 * sin], axis=-1).astype(x.dtype)
    return jnp.concatenate([rot, x[..., ROT_DIM:]], axis=-1)


def _masked_softmax(s, mask):
    s = jnp.where(mask, s, NEG_BIG)
    m = jnp.max(s, axis=-1, keepdims=True)
    p = jnp.where(mask, jnp.exp(s - m), 0.0)
    denom = jnp.sum(p, axis=-1, keepdims=True)
    return p / jnp.where(denom > 0, denom, 1.0)


def hgrn2_mixer(q, f_logit, i, lb):
    B, T, _ = q.shape
    H, dk, C, c = GROUP_HEADS, HEAD_DIM, HG_CHUNK, HG_SUB
    n_sub = C // c
    q = jax.nn.silu(q.astype(jnp.float32))
    f_logit = f_logit.astype(jnp.float32)
    log_f = jnp.log(lb + (1.0 - lb) * jax.nn.sigmoid(f_logit))
    k = (1.0 - lb) * jax.nn.sigmoid(-f_logit)
    v = i.astype(jnp.float32)

    def chunks(a):
        return a.reshape(B, T // C, C, H, dk).transpose(1, 0, 3, 2, 4)

    sub_before = jnp.tril(jnp.ones((n_sub, n_sub), dtype=bool), -1)
    causal = jnp.tril(jnp.ones((c, c), dtype=bool))

    def step(S, xs):
        qc, kc, vc, gc = xs
        b = jnp.cumsum(gc, axis=2)
        o_inter = jnp.einsum('bhck,bhkv->bhcv', qc * jnp.exp(b), S)
        qs, ks, vs, bs = (a.reshape(B, H, n_sub, c, dk) for a in (qc, kc, vc, b))
        b_ref = jnp.concatenate([jnp.zeros_like(bs[:, :, :1, 0]), bs[:, :, :-1, -1]], axis=2)
        q_off = qs * jnp.exp(bs - b_ref[:, :, :, None])
        e_off = jnp.where(sub_before[:, :, None, None],
                          b_ref[:, :, :, None, None] - bs[:, :, None], NEG_BIG)
        a_off = jnp.einsum('bhitd,bhijsd->bhijts', q_off, ks[:, :, None] * jnp.exp(e_off))
        e_diag = jnp.where(causal[:, :, None],
                           bs[:, :, :, :, None] - bs[:, :, :, None], NEG_BIG)
        a_diag = jnp.einsum('bhntd,bhntsd,bhnsd->bhnts', qs, jnp.exp(e_diag), ks)
        o_intra = (jnp.einsum('bhijts,bhjsv->bhitv', a_off, vs)
                   + jnp.einsum('bhnts,bhnsv->bhntv', a_diag, vs))
        o = o_inter + o_intra.reshape(B, H, C, dk)
        b_last = b[:, :, -1]
        S = (jnp.exp(b_last)[..., None] * S
             + jnp.einsum('bhck,bhcv->bhkv', kc * jnp.exp(b_last[:, :, None] - b), vc))
        return S, o

    S0 = jnp.zeros((B, H, dk, dk), jnp.float32)
    _, o = lax.scan(step, S0, (chunks(q), chunks(k), chunks(v), chunks(log_f)))
    return o.transpose(1, 0, 3, 2, 4).reshape(B, T, H * dk)


def fox_mixer(q, k, v, f_logit, fb):
    B, T, _ = q.shape
    q, k, v = (_to_heads(a, GROUP_HEADS) for a in (q, k, v))
    log_f = jax.nn.log_sigmoid(f_logit.astype(jnp.float32) + fb.astype(jnp.float32))
    cum = jnp.cumsum(log_f, axis=1).transpose(0, 2, 1)
    outs = []
    for blk in range(T // Q_BLOCK):
        t0, t1 = blk * Q_BLOCK, (blk + 1) * Q_BLOCK
        s = jnp.einsum('bhqd,bhkd->bhqk', q[:, :, t0:t1], k[:, :, :t1]).astype(jnp.float32) * ATTN_SCALE
        s = s + cum[:, :, t0:t1, None] - cum[:, :, None, :t1]
        mask = jnp.arange(t0, t1)[:, None] >= jnp.arange(t1)[None, :]
        p = _masked_softmax(s, mask)
        outs.append(jnp.einsum('bhqk,bhkd->bhqd', p.astype(v.dtype), v[:, :, :t1]))
    return _from_heads(jnp.concatenate(outs, axis=2))


def stick_breaking_mixer(q, k, v):
    B, T, _ = q.shape
    q, k, v = (_to_heads(a, GROUP_HEADS) for a in (q, k, v))
    outs = []
    for blk in range(T // Q_BLOCK):
        t0, t1 = blk * Q_BLOCK, (blk + 1) * Q_BLOCK
        z = jnp.einsum('bhqd,bhkd->bhqk', q[:, :, t0:t1], k[:, :, :t1]).astype(jnp.float32) * ATTN_SCALE
        mask = jnp.arange(t1)[None, :] < jnp.arange(t0, t1)[:, None]
        log_om = jnp.where(mask, jax.nn.log_sigmoid(-z), 0.0)
        suffix = lax.cumsum(log_om, axis=3, reverse=True) - log_om
        a = jnp.where(mask, jnp.exp(jnp.where(mask, jax.nn.log_sigmoid(z) + suffix, 0.0)), 0.0)
        outs.append(jnp.einsum('bhqk,bhkd->bhqd', a.astype(v.dtype), v[:, :, :t1]))
    return _from_heads(jnp.concatenate(outs, axis=2))


def _nsa_compress(a, pe, w1, w2):
    B, T, d = a.shape
    n_cmp = (T - NSA_CMP_LEN) // NSA_CMP_STRIDE + 1
    idx = jnp.arange(n_cmp)[:, None] * NSA_CMP_STRIDE + jnp.arange(NSA_CMP_LEN)[None, :]
    blocks = a[:, idx] + pe
    return jax.nn.silu(blocks.reshape(B, n_cmp, NSA_CMP_LEN * d) @ w1) @ w2


def nsa_mixer(q, kc, vc, ks, vs, kw, vw, gate_logits,
              pe_k, w1_k, w2_k, pe_v, w1_v, w2_v):
    B, T, _ = q.shape
    H, d = GROUP_HEADS, HEAD_DIM
    pos = jnp.arange(T)
    q = _partial_rope(_to_heads(q, H), pos)

    n_cmp = (T - NSA_CMP_LEN) // NSA_CMP_STRIDE + 1
    cmp_end = jnp.arange(n_cmp) * NSA_CMP_STRIDE + NSA_CMP_LEN - 1
    k_cmp = _partial_rope(_nsa_compress(kc, pe_k, w1_k, w2_k), cmp_end)
    v_cmp = _nsa_compress(vc, pe_v, w1_v, w2_v)
    s_cmp = jnp.einsum('bhtd,bnd->bhtn', q, k_cmp).astype(jnp.float32) * ATTN_SCALE
    p_cmp = _masked_softmax(s_cmp, cmp_end[None, :] <= pos[:, None])
    o_cmp = jnp.einsum('bhtn,bnd->bhtd', p_cmp, v_cmp)

    n_sel = T // NSA_SEL_BLOCK
    top_n = min(NSA_TOP_N, n_sel)
    c_start = jnp.arange(n_cmp)[:, None] * NSA_CMP_STRIDE
    s_start = jnp.arange(n_sel)[None, :] * NSA_SEL_BLOCK
    overlap = jnp.clip(jnp.minimum(c_start + NSA_CMP_LEN, s_start + NSA_SEL_BLOCK)
                       - jnp.maximum(c_start, s_start), 0, None).astype(jnp.float32) / NSA_CMP_LEN
    importance = jnp.einsum('bhtn,nj->btj', p_cmp, overlap)
    q_blk = pos // NSA_SEL_BLOCK
    blk_ids = jnp.arange(n_sel)
    forced = ((blk_ids[None, :] == 0) | (blk_ids[None, :] == q_blk[:, None])
              | (blk_ids[None, :] == q_blk[:, None] - 1))
    score = jnp.where(forced, NSA_FORCE, importance)
    score = jnp.where(blk_ids[None, :] <= q_blk[:, None], score, -NSA_FORCE)
    _, sel_idx = lax.top_k(score, top_n)

    ks_r = _partial_rope(ks, pos)
    k_blocks = ks_r.reshape(B, n_sel, NSA_SEL_BLOCK, d)
    v_blocks = vs.reshape(B, n_sel, NSA_SEL_BLOCK, d)
    qb_n = NSA_SEL_Q_BLOCK
    n_qb = T // qb_n
    bidx = jnp.arange(B)[:, None, None]

    def sel_block(args):
        qb, idxb, posb = args
        kg = k_blocks[bidx, idxb]
        vg = v_blocks[bidx, idxb]
        s = jnp.einsum('bhqd,bqnkd->bhqnk', qb, kg).astype(jnp.float32) * ATTN_SCALE
        key_pos = idxb[..., None] * NSA_SEL_BLOCK + jnp.arange(NSA_SEL_BLOCK)
        mask = key_pos <= posb[None, :, None, None]
        p = _masked_softmax(s.reshape(B, H, qb_n, top_n * NSA_SEL_BLOCK),
                            mask.reshape(B, 1, qb_n, top_n * NSA_SEL_BLOCK))
        return jnp.einsum('bhqm,bqmd->bhqd', p, vg.reshape(B, qb_n, top_n * NSA_SEL_BLOCK, d))

    o_sel = lax.map(sel_block, (q.reshape(B, H, n_qb, qb_n, d).transpose(2, 0, 1, 3, 4),
                                sel_idx.reshape(B, n_qb, qb_n, top_n).transpose(1, 0, 2, 3),
                                pos.reshape(n_qb, qb_n)))
    o_sel = o_sel.transpose(1, 2, 0, 3, 4).reshape(B, H, T, d)

    W = NSA_WINDOW
    kw_pad = jnp.pad(_partial_rope(kw, pos), ((0, 0), (W, 0), (0, 0)))
    vw_pad = jnp.pad(vw, ((0, 0), (W, 0), (0, 0)))
    n_wb = T // Q_BLOCK

    def win_block(args):
        qb, t0 = args
        kb = lax.dynamic_slice_in_dim(kw_pad, t0, W + Q_BLOCK, axis=1)
        vb = lax.dynamic_slice_in_dim(vw_pad, t0, W + Q_BLOCK, axis=1)
        s = jnp.einsum('bhqd,bkd->bhqk', qb, kb).astype(jnp.float32) * ATTN_SCALE
        qpos = t0 + jnp.arange(Q_BLOCK)
        kpos = t0 - W + jnp.arange(W + Q_BLOCK)
        mask = ((kpos[None, :] <= qpos[:, None]) & (kpos[None, :] > qpos[:, None] - W)
                & (kpos[None, :] >= 0))
        p = _masked_softmax(s, mask)
        return jnp.einsum('bhqk,bkd->bhqd', p, vb)

    o_win = lax.map(win_block, (q.reshape(B, H, n_wb, Q_BLOCK, d).transpose(2, 0, 1, 3, 4),
                                jnp.arange(n_wb) * Q_BLOCK))
    o_win = o_win.transpose(1, 2, 0, 3, 4).reshape(B, H, T, d)

    g = jax.nn.sigmoid(gate_logits.astype(jnp.float32)).reshape(B, T, 3, H).transpose(2, 0, 3, 1)
    o = g[0][..., None] * o_cmp + g[1][..., None] * o_sel + g[2][..., None] * o_win
    return _from_heads(o)


def setup_inputs(seed: int = 0) -> dict:
    key = jax.random.key(seed)
    ks = jax.random.split(key, 14)
    f32 = jnp.float32
    nrm = jax.random.normal
    cmp_in = NSA_CMP_LEN * HEAD_DIM
    return {
        "x": nrm(ks[0], (BATCH, SEQ, D_MODEL), f32),
        "norm_g": 1.0 + 0.02 * nrm(ks[1], (DEPTH, D_MODEL), f32),
        "w_in": nrm(ks[2], (DEPTH, D_MODEL, N_IN), f32) * D_MODEL ** -0.5,
        "hgrn_lb_logits": 0.5 * nrm(ks[3], (DEPTH, GROUP_W), f32),
        "fox_fb": 2.0 + 0.5 * nrm(ks[4], (DEPTH, GROUP_HEADS), f32),
        "nsa_cmp_pe_k": 0.1 * nrm(ks[5], (DEPTH, NSA_CMP_LEN, HEAD_DIM), f32),
        "nsa_cmp_w1_k": nrm(ks[6], (DEPTH, cmp_in, NSA_CMP_HIDDEN), f32) * cmp_in ** -0.5,
        "nsa_cmp_w2_k": nrm(ks[7], (DEPTH, NSA_CMP_HIDDEN, HEAD_DIM), f32) * NSA_CMP_HIDDEN ** -0.5,
        "nsa_cmp_pe_v": 0.1 * nrm(ks[8], (DEPTH, NSA_CMP_LEN, HEAD_DIM), f32),
        "nsa_cmp_w1_v": nrm(ks[9], (DEPTH, cmp_in, NSA_CMP_HIDDEN), f32) * cmp_in ** -0.5,
        "nsa_cmp_w2_v": nrm(ks[10], (DEPTH, NSA_CMP_HIDDEN, HEAD_DIM), f32) * NSA_CMP_HIDDEN ** -0.5,
        "out_norm_g": 1.0 + 0.02 * nrm(ks[11], (DEPTH, D_MIX), f32),
        "w_out": nrm(ks[12], (DEPTH, D_MIX, D_MODEL), f32) * D_MIX ** -0.5,
        "final_norm_g": 1.0 + 0.02 * nrm(ks[13], (D_MODEL,), f32),
    }


def reference(x, norm_g, w_in, hgrn_lb_logits, fox_fb, nsa_cmp_pe_k, nsa_cmp_w1_k, nsa_cmp_w2_k,
              nsa_cmp_pe_v, nsa_cmp_w1_v, nsa_cmp_w2_v, out_norm_g, w_out, final_norm_g):
    B, T, _ = x.shape
    lb_sm = jax.nn.softmax(hgrn_lb_logits.astype(jnp.float32), axis=0)
    lb_all = jnp.cumsum(lb_sm, axis=0) - lb_sm[0:1]
    split_at = np.cumsum(IN_WIDTHS)[:-1].tolist()
    for l in range(DEPTH):
        h = _rmsnorm(x, norm_g[l])
        (hg_q, hg_f, hg_i, fx_q, fx_k, fx_v, fx_f, sb_q, sb_k, sb_v,
         ns_q, ns_kc, ns_vc, ns_ks, ns_vs, ns_kw, ns_vw, ns_g, gate) = jnp.split(h @ w_in[l], split_at, axis=-1)
        y = jnp.concatenate([
            hgrn2_mixer(hg_q, hg_f, hg_i, lb_all[l]).astype(x.dtype),
            fox_mixer(fx_q, fx_k, fx_v, fx_f, fox_fb[l]).astype(x.dtype),
            stick_breaking_mixer(sb_q, sb_k, sb_v).astype(x.dtype),
            nsa_mixer(ns_q, ns_kc, ns_vc, ns_ks, ns_vs, ns_kw, ns_vw, ns_g,
                      nsa_cmp_pe_k[l], nsa_cmp_w1_k[l], nsa_cmp_w2_k[l],
                      nsa_cmp_pe_v[l], nsa_cmp_w1_v[l], nsa_cmp_w2_v[l]).astype(x.dtype),
        ], axis=-1)
        yh = y.reshape(B, T, N_MIX_HEADS, HEAD_DIM).astype(jnp.float32)
        yh = yh * lax.rsqrt(jnp.mean(yh * yh, axis=-1, keepdims=True) + EPS)
        y = (yh.reshape(B, T, D_MIX) * out_norm_g[l].astype(jnp.float32)
             * jax.nn.silu(gate.astype(jnp.float32)))
        x = x + y.astype(x.dtype) @ w_out[l]
    return _rmsnorm(x, final_norm_g)
```

```python
import functools

import numpy as np
import jax
import jax.numpy as jnp
from jax import lax
from jax.experimental import pallas as pl
from jax.experimental.pallas import tpu as pltpu

F32 = jnp.float32
BF16 = jnp.bfloat16

D_MODEL = 1024
HEAD_DIM = 64
GROUP_W = 256
GROUP_HEADS = 4
ATTN_SCALE = HEAD_DIM ** -0.5
ROPE_THETA = 500000.0
ROT_DIM = HEAD_DIM // 4
HG_CHUNK = 64
NSA_CMP_LEN = 32
NSA_CMP_STRIDE = 16
NSA_SEL_BLOCK = 64
NSA_TOP_N = 16
NSA_WINDOW = 512
NSA_FORCE = 1.0e4
NEG_BIG = -1.0e30
SEL_OFF = -(2.0 ** 100)
EPS = 1e-6

LANES = 128
VMEM_LIMIT = 56 * 1024 * 1024

IN_WIDTHS = (GROUP_W,) * 3 + (GROUP_W,) * 3 + (GROUP_HEADS,) + (GROUP_W,) * 3 + (GROUP_W,) \
    + (HEAD_DIM,) * 6 + (3 * GROUP_HEADS,) + (D_MODEL,)
IN_OFFS = np.concatenate([[0], np.cumsum(IN_WIDTHS)]).tolist()


def _nn(a, b):
    return jnp.dot(a, b, preferred_element_type=F32)


def _nt(a, b):
    return lax.dot_general(a, b, (((1,), (1,)), ((), ())), preferred_element_type=F32)


def _tn(a, b):
    return lax.dot_general(a, b, (((0,), (0,)), ((), ())), preferred_element_type=F32)


def _iota(shape, dim):
    return lax.broadcasted_iota(jnp.int32, shape, dim)


def _split2(x):
    hi = x.astype(BF16)
    lo = (x - hi.astype(F32)).astype(BF16)
    return hi, lo


def _split3(x):
    p1 = x.astype(BF16).astype(F32)
    r = x - p1
    p2 = r.astype(BF16).astype(F32)
    p3 = (r - p2).astype(BF16).astype(F32)
    return p1, p2, p3


def _sigmoid(x):
    return 1.0 / (1.0 + jnp.exp(-x))


def _log_sigmoid(x):
    return jnp.minimum(x, 0.0) - jnp.log(1.0 + jnp.exp(-jnp.abs(x)))


def _div(x, n):
    assert n & (n - 1) == 0
    return x >> (n.bit_length() - 1)


def _rows(i, n):
    return pl.ds(pl.multiple_of(i * n, n), n)


def _head_to_slot(x, h):
    return x if h % 2 == 0 else pltpu.roll(x, HEAD_DIM, 1)


def _proj_kernel(x_ref, g_ref, w_hg, w_fx, w_sb, w_ns, w_gate, w_sm,
                 o_hg, o_fx, o_sb, o_ns, o_gate, o_sm):
    x = x_ref[...]
    h = x * lax.rsqrt(jnp.mean(x * x, axis=-1, keepdims=True) + EPS) * g_ref[...]
    hb = h.astype(BF16)
    for w, o in ((w_hg, o_hg), (w_fx, o_fx), (w_sb, o_sb), (w_ns, o_ns), (w_gate, o_gate), (w_sm, o_sm)):
        o[...] = _nn(hb, w[...])


def _project(x2, norm_g, ws, tm):
    n = x2.shape[0]
    widths = [w.shape[1] for w in ws]
    return pl.pallas_call(
        _proj_kernel,
        grid=(n // tm,),
        in_specs=[pl.BlockSpec((tm, D_MODEL), lambda i: (i, 0)),
                  pl.BlockSpec((1, D_MODEL), lambda i: (0, 0))]
        + [pl.BlockSpec((D_MODEL, wd), lambda i: (0, 0)) for wd in widths],
        out_specs=[pl.BlockSpec((tm, wd), lambda i: (i, 0)) for wd in widths],
        out_shape=[jax.ShapeDtypeStruct((n, wd), F32) for wd in widths],
        compiler_params=pltpu.CompilerParams(dimension_semantics=("arbitrary",),
                                             vmem_limit_bytes=VMEM_LIMIT),
        name="proj",
    )(x2, norm_g.reshape(1, D_MODEL), *ws)


def _fox_kernel(g_ref, sm_ref, fb_ref, o_ref, cum_ref, qa_ref, ka_ref, va_ref, *, T, TQ):
    H = GROUP_HEADS
    RB = 256
    tri = jnp.where(_iota((LANES, LANES), 0) >= _iota((LANES, LANES), 1), 1.0, 0.0).astype(BF16)
    fb = fb_ref[...]

    def cum_body(i, carry):
        r = _rows(i, LANES)
        p1, p2, p3 = _split3(_log_sigmoid(sm_ref[r, :] + fb))
        c = (_nn(tri, p1.astype(BF16)) + _nn(tri, p2.astype(BF16)) + _nn(tri, p3.astype(BF16))) + carry
        cum_ref[r, :] = c
        return c[LANES - 1:LANES, :]

    lax.fori_loop(0, T // LANES, cum_body, jnp.zeros((1, LANES), F32))

    lane = _iota((RB, LANES), 1)

    def build_body(i, _):
        r = _rows(i, RB)
        cum = cum_ref[r, :]
        for h in range(H):
            j = h // 2
            q = _head_to_slot(g_ref[r, j * LANES:(j + 1) * LANES], h) * ATTN_SCALE
            k = _head_to_slot(g_ref[r, GROUP_W + j * LANES:GROUP_W + (j + 1) * LANES], h)
            v = _head_to_slot(g_ref[r, 2 * GROUP_W + j * LANES:2 * GROUP_W + (j + 1) * LANES], h)
            c1, c2, c3 = _split3(cum[:, h:h + 1])
            qa = jnp.where(lane < 64, q, 0.0)
            ka = jnp.where(lane < 64, k, 0.0)
            for p, c in enumerate((c1, c2, c3)):
                qa = jnp.where(lane == 64 + p, c, qa)
                qa = jnp.where(lane == 67 + p, 1.0, qa)
                ka = jnp.where(lane == 64 + p, 1.0, ka)
                ka = jnp.where(lane == 67 + p, -c, ka)
            va = jnp.where(lane < 64, v, jnp.where(lane == 64, 1.0, 0.0))
            qa_ref[h, r, :] = qa.astype(BF16)
            ka_ref[h, r, :] = ka.astype(BF16)
            va_ref[h, r, :] = va.astype(BF16)
        return 0

    lax.fori_loop(0, T // RB, build_body, 0)

    causal = _iota((TQ, TQ), 1) <= _iota((TQ, TQ), 0)

    def q_body(i, _):
        rq = _rows(i, TQ)
        outs = []
        for h in range(H):
            qa = qa_ref[h, rq, :]

            def k_step(j, carry, masked):
                m, acc = carry
                rk = _rows(j, TQ)
                s = _nt(qa, ka_ref[h, rk, :])
                if masked:
                    s = jnp.where(causal, s, NEG_BIG)
                m_new = jnp.maximum(m, jnp.max(s, axis=-1, keepdims=True))
                p = jnp.exp(s - m_new)
                acc = jnp.exp(m - m_new) * acc + _nn(p.astype(BF16), va_ref[h, rk, :])
                return m_new, acc

            carry = (jnp.full((TQ, 1), NEG_BIG, F32), jnp.zeros((TQ, LANES), F32))
            carry = lax.fori_loop(0, i, functools.partial(k_step, masked=False), carry)
            _, acc = k_step(i, carry, True)
            outs.append(acc[:, :HEAD_DIM] / acc[:, HEAD_DIM:HEAD_DIM + 1])
        o_ref[rq, :] = jnp.concatenate(outs, axis=1)
        return 0

    lax.fori_loop(0, T // TQ, q_body, 0)


def _fox(g_fx, g_sm, fb, B, T):
    TQ = 256
    kern = functools.partial(_fox_kernel, T=T, TQ=TQ)
    return pl.pallas_call(
        kern,
        grid=(B,),
        in_specs=[pl.BlockSpec((None, T, 3 * GROUP_W), lambda b: (b, 0, 0)),
                  pl.BlockSpec((None, T, LANES), lambda b: (b, 0, 0)),
                  pl.BlockSpec((1, LANES), lambda b: (0, 0))],
        out_specs=pl.BlockSpec((None, T, GROUP_W), lambda b: (b, 0, 0)),
        out_shape=jax.ShapeDtypeStruct((B, T, GROUP_W), F32),
        scratch_shapes=[pltpu.VMEM((T, LANES), F32)]
        + [pltpu.VMEM((GROUP_HEADS, T, LANES), BF16) for _ in range(3)],
        compiler_params=pltpu.CompilerParams(dimension_semantics=("arbitrary",),
                                             vmem_limit_bytes=VMEM_LIMIT),
        name="fox",
    )(g_fx, g_sm, fb)


def _sb_kernel(g_ref, o_ref, qa_ref, ka_ref, va_ref, *, T, TQ):
    H = GROUP_HEADS
    RB = 256
    TK = LANES
    lane = _iota((RB, LANES), 1)

    def build_body(i, _):
        r = _rows(i, RB)
        for h in range(H):
            j = h // 2
            q = _head_to_slot(g_ref[r, j * LANES:(j + 1) * LANES], h) * ATTN_SCALE
            k = _head_to_slot(g_ref[r, GROUP_W + j * LANES:GROUP_W + (j + 1) * LANES], h)
            v = _head_to_slot(g_ref[r, 2 * GROUP_W + j * LANES:2 * GROUP_W + (j + 1) * LANES], h)
            qa_ref[h, r, :] = jnp.where(lane < 64, q, 0.0).astype(BF16)
            ka_ref[h, r, :] = jnp.where(lane < 64, k, 0.0).astype(BF16)
            va_ref[h, r, :] = jnp.where(lane < 64, v, 0.0).astype(BF16)
        return 0

    lax.fori_loop(0, T // RB, build_body, 0)

    wr = _iota((2 * TK, 2 * TK), 0) & (TK - 1)
    wc = _iota((2 * TK, 2 * TK), 1)
    suf_w = jnp.where((wc >= TK) | (wr > wc), 1.0, 0.0).astype(BF16)
    row = _iota((TQ, TK), 0)
    col = _iota((TQ, TK), 1)

    def q_body(i, _):
        rq = _rows(i, TQ)
        outs = []
        for h in range(H):
            qa = qa_ref[h, rq, :]

            def k_step(j, carry, masked):
                c, acc = carry
                rk = _rows(j, TK)
                z = _nt(qa, ka_ref[h, rk, :])
                ls = jnp.minimum(z, 0.0) - jnp.log(1.0 + jnp.exp(-jnp.abs(z)))
                lom = ls - z
                if masked:
                    msk = (j * TK + col) < (i * TQ + row)
                    lom = jnp.where(msk, lom, 0.0)
                hi, lo = _split2(lom)
                cs = _nn(jnp.concatenate([hi, lo], axis=1), suf_w)
                a = jnp.exp(ls + cs[:, :TK] + c)
                if masked:
                    a = jnp.where(msk, a, 0.0)
                acc = acc + _nn(a.astype(BF16), va_ref[h, rk, :])
                return c + cs[:, TK:], acc

            carry = (jnp.zeros((TQ, TK), F32), jnp.zeros((TQ, LANES), F32))
            nd = TQ // TK
            for d in range(nd):
                carry = k_step(i * nd + (nd - 1 - d), carry, True)
            carry = lax.fori_loop(0, i * nd, lambda jj, cr: k_step(i * nd - 1 - jj, cr, False), carry)
            outs.append(carry[1][:, :HEAD_DIM])
        o_ref[rq, :] = jnp.concatenate(outs, axis=1)
        return 0

    lax.fori_loop(0, T // TQ, q_body, 0)


def _stick_breaking(g_sb, B, T):
    TQ = 256
    kern = functools.partial(_sb_kernel, T=T, TQ=TQ)
    return pl.pallas_call(
        kern,
        grid=(B,),
        in_specs=[pl.BlockSpec((None, T, 3 * GROUP_W), lambda b: (b, 0, 0))],
        out_specs=pl.BlockSpec((None, T, GROUP_W), lambda b: (b, 0, 0)),
        out_shape=jax.ShapeDtypeStruct((B, T, GROUP_W), F32),
        scratch_shapes=[pltpu.VMEM((GROUP_HEADS, T, LANES), BF16) for _ in range(3)],
        compiler_params=pltpu.CompilerParams(dimension_semantics=("arbitrary",),
                                             vmem_limit_bytes=VMEM_LIMIT),
        name="stickbreak",
    )(g_sb)


def _hgrn_decay_matrix(C):
    nl = C.bit_length() - 1
    t = _iota((C, C), 0)
    u = _iota((C, C), 1)
    blocks = [u <= t, u > t]
    q_blocks, k_blocks = [], []
    for l in range(nl):
        m = C >> (l + 1)
        bnd = (t & ~(2 * m - 1)) + (m - 1)
        q_blocks.append((u > bnd) & (u <= t))
        k_blocks.append((u > t) & (u <= bnd))
    d = jnp.concatenate([jnp.where(b, 1.0, 0.0) for b in blocks + q_blocks + k_blocks], axis=0).astype(BF16)
    return jnp.concatenate([d, d, d], axis=1)


def _hgrn_kernel(g_ref, lb_ref, o_ref, st_ref, *, T, C, layer):
    H = GROUP_HEADS
    nl = C.bit_length() - 1
    lg = lb_ref[...]
    e = jnp.exp(lg - jnp.max(lg, axis=0, keepdims=True))
    sm = e / jnp.sum(e, axis=0, keepdims=True)
    lb = jnp.sum(sm[0:layer + 1, :], axis=0, keepdims=True) - sm[0:1, :]

    dmat = _hgrn_decay_matrix(C)
    lane_head = _div(_iota((C, GROUP_W), 1), HEAD_DIM)
    lane_head4 = _div(_iota((H * C, GROUP_W), 1), HEAD_DIM)
    row_head4 = _div(_iota((H * C, GROUP_W), 0), C)
    head_sel = lane_head4 == row_head4
    tq_c = _iota((C, GROUP_W), 0)
    t4 = _iota((H * C, C), 0) & (C - 1)
    s4 = _iota((H * C, C), 1)
    bd_mask = _div(_iota((GROUP_W, GROUP_W), 0), HEAD_DIM) == _div(_iota((GROUP_W, GROUP_W), 1), HEAD_DIM)
    st_ref[...] = jnp.zeros((GROUP_W, GROUP_W), F32)

    def chunk(ci, _):
        r = _rows(ci, C)
        qraw = g_ref[r, 0:GROUP_W]
        f = g_ref[r, GROUP_W:2 * GROUP_W]
        v = g_ref[r, 2 * GROUP_W:3 * GROUP_W].astype(BF16)
        q = qraw * _sigmoid(qraw)
        g = jnp.log(lb + (1.0 - lb) * _sigmoid(f))
        k = (1.0 - lb) * _sigmoid(-f)
        g1, g2, g3 = _split3(g)
        ex = _nn(dmat, jnp.concatenate([g1.astype(BF16), g2.astype(BF16), g3.astype(BF16)], axis=0))
        b = ex[0:C]
        st = st_ref[...]
        o_inter = _nt((q * jnp.exp(b)).astype(BF16), st.astype(BF16))

        a = jnp.zeros((H * C, C), F32)
        for l in range(nl + 1):
            if l < nl:
                m = C >> (l + 1)
                dq = ex[(2 + l) * C:(3 + l) * C]
                dk = ex[(2 + nl + l) * C:(3 + nl + l) * C]
                qs = q * jnp.where((tq_c & m) != 0, jnp.exp(dq), 0.0)
                ks = k * jnp.where((tq_c & m) == 0, jnp.exp(dk), 0.0)
                pair = (t4 & ~(2 * m - 1)) == (s4 & ~(2 * m - 1))
            else:
                qs, ks = q, k
                pair = t4 == s4
            qsb = qs.astype(BF16)
            qst = jnp.where(head_sel, jnp.concatenate([qsb] * H, axis=0), jnp.zeros((), BF16))
            a = a + jnp.where(pair, _nt(qst, ks.astype(BF16)), 0.0)
        rr = _nn(a.astype(BF16), v)
        o_intra = rr[0:C]
        for h in range(1, H):
            o_intra = jnp.where(lane_head == h, rr[h * C:(h + 1) * C], o_intra)
        o_ref[r, :] = o_inter + o_intra

        kd = (k * jnp.exp(ex[C:2 * C])).astype(BF16)
        upd = _tn(v, kd)
        st_ref[...] = st * jnp.exp(b[C - 1:C, :]) + jnp.where(bd_mask, upd, 0.0)
        return 0

    lax.fori_loop(0, T // C, chunk, 0)


def _hgrn2(g_hg, lb_logits, layer, B, T):
    depth = lb_logits.shape[0]
    kern = functools.partial(_hgrn_kernel, T=T, C=HG_CHUNK, layer=layer)
    return pl.pallas_call(
        kern,
        grid=(B,),
        in_specs=[pl.BlockSpec((None, T, 3 * GROUP_W), lambda b: (b, 0, 0)),
                  pl.BlockSpec((depth, GROUP_W), lambda b: (0, 0))],
        out_specs=pl.BlockSpec((None, T, GROUP_W), lambda b: (b, 0, 0)),
        out_shape=jax.ShapeDtypeStruct((B, T, GROUP_W), F32),
        scratch_shapes=[pltpu.VMEM((GROUP_W, GROUP_W), F32)],
        compiler_params=pltpu.CompilerParams(dimension_semantics=("arbitrary",),
                                             vmem_limit_bytes=VMEM_LIMIT),
        name="hgrn2",
    )(g_hg, lb_logits)


def _rope(x, cos, sin, lane):
    swapped = jnp.where((lane & (HEAD_DIM - 1)) < ROT_DIM // 2,
                        pltpu.roll(x, LANES - ROT_DIM // 2, 1), pltpu.roll(x, ROT_DIM // 2, 1))
    return x * cos + swapped * sin


def _nsa_kernel(g_ref, sm_ref, cosq_ref, sinq_ref, cosk_ref, sink_ref, cosc_ref, sinc_ref,
                pe_ref, w1_ref, w2_ref, o_ref,
                cv_ref, ablk_ref, qa_ref, kcmp_ref, vcmp_ref, ksel_ref, vsel_ref, kwin_ref, vwin_ref,
                *, T, TQ):
    H = GROUP_HEADS
    RB = 256
    NC = T // NSA_CMP_STRIDE
    NSEL = T // NSA_SEL_BLOCK
    TOPN = min(NSA_TOP_N, NSEL)
    TK = LANES
    WT = NSA_WINDOW // TK
    lane = _iota((RB, LANES), 1)
    grow = _iota((RB, LANES), 0)

    def build_body(i, _):
        r = _rows(i, RB)
        cosq, sinq = cosq_ref[r, :], sinq_ref[r, :]
        cosk, sink = cosk_ref[r, :], sink_ref[r, :]
        for j in range(2):
            xr = _rope(g_ref[r, j * LANES:(j + 1) * LANES], cosq, sinq, lane) * ATTN_SCALE
            for h in (2 * j, 2 * j + 1):
                qa_ref[h, r, :] = jnp.where(lane < 64, _head_to_slot(xr, h), 0.0).astype(BF16)
        cv_ref[r, :] = g_ref[r, GROUP_W:GROUP_W + LANES]
        ksvs = g_ref[r, GROUP_W + LANES:GROUP_W + 2 * LANES]
        kwvw = g_ref[r, GROUP_W + 2 * LANES:GROUP_W + 3 * LANES]
        blk = _div(i * RB + grow, NSA_SEL_BLOCK)
        onehot = jnp.where((lane >= 64) & (lane - 64 == blk), 1.0, 0.0)
        ones_col = jnp.where(lane == 64, 1.0, 0.0)
        ksel_ref[r, :] = jnp.where(lane < 64, _rope(ksvs, cosk, sink, lane), onehot).astype(BF16)
        vsel_ref[r, :] = jnp.where(lane < 64, pltpu.roll(ksvs, 64, 1), ones_col).astype(BF16)
        kwin_ref[r, :] = jnp.where(lane < 64, _rope(kwvw, cosk, sink, lane), 0.0).astype(BF16)
        vwin_ref[r, :] = jnp.where(lane < 64, pltpu.roll(kwvw, 64, 1), ones_col).astype(BF16)
        return 0

    lax.fori_loop(0, T // RB, build_body, 0)
    cv_ref[T:T + 2 * NSA_CMP_STRIDE, :] = jnp.zeros((2 * NSA_CMP_STRIDE, LANES), F32)

    for l in range(NSA_CMP_LEN):
        blk_l = cv_ref[pl.ds(l, NC, stride=NSA_CMP_STRIDE), :] + pe_ref[l:l + 1, :]
        ablk_ref[:, l * LANES:(l + 1) * LANES] = blk_l.astype(BF16)
    hid = _nn(ablk_ref[...], w1_ref[...])
    hid = hid * _sigmoid(hid)
    kv = _nn(hid.astype(BF16), w2_ref[...])
    lane_c = _iota((NC, LANES), 1)
    kcmp_ref[...] = jnp.where(lane_c < 64, _rope(kv, cosc_ref[...], sinc_ref[...], lane_c), 0.0).astype(BF16)
    vcmp_ref[...] = jnp.where(lane_c < 64, pltpu.roll(kv, 64, 1), 0.0).astype(BF16)

    on = _iota((2 * NC, LANES), 0) & (NC - 1)
    oj = _iota((2 * NC, LANES), 1) - 64
    ov = jnp.clip(jnp.minimum(on * NSA_CMP_STRIDE + NSA_CMP_LEN, oj * NSA_SEL_BLOCK + NSA_SEL_BLOCK)
                  - jnp.maximum(on * NSA_CMP_STRIDE, oj * NSA_SEL_BLOCK), 0, None).astype(F32) / NSA_CMP_LEN
    ov = jnp.where((oj >= 0) & (oj < NSEL) & (on < NC - 1), ov, 0.0).astype(BF16)

    row4 = _iota((H * TQ, LANES), 0) & (TQ - 1)
    col4 = _iota((H * TQ, LANES), 1)
    lane_q = _iota((TQ, LANES), 1)
    row_q = _iota((TQ, LANES), 0)
    cmp_end = _iota((H * TQ, NC), 1) * NSA_CMP_STRIDE + (NSA_CMP_LEN - 1)
    rowc = _iota((H * TQ, NC), 0) & (TQ - 1)

    def q_body(i, _):
        rq = _rows(i, TQ)
        t0 = i * TQ
        qst = jnp.concatenate([qa_ref[h, rq, :] for h in range(H)], axis=0)

        valid = cmp_end <= (t0 + rowc)
        s = jnp.where(valid, _nt(qst, kcmp_ref[...]), NEG_BIG)
        p = jnp.where(valid, jnp.exp(s - jnp.max(s, axis=-1, keepdims=True)), 0.0)
        den = jnp.sum(p, axis=-1, keepdims=True)
        p = p / jnp.where(den > 0, den, 1.0)
        o_cmp = _nn(p.astype(BF16), vcmp_ref[...])
        psum = p[0:TQ]
        for h in range(1, H):
            psum = psum + p[h * TQ:(h + 1) * TQ]
        hi, lo = _split2(psum)
        imp = _nn(jnp.concatenate([hi, lo], axis=1), ov)

        jl = lane_q - 64
        qblk = _div(t0 + row_q, NSA_SEL_BLOCK)
        forced = (jl == 0) | (jl == qblk) | (jl == qblk - 1)
        score = jnp.where(forced, NSA_FORCE, imp)
        score = jnp.where(jl <= qblk, score, -NSA_FORCE)
        rank = jnp.zeros((TQ, LANES), F32)
        for ii in range(NSEL):
            ci = score[:, 64 + ii:65 + ii]
            rank = rank + jnp.where((ci > score) | ((ci == score) & (ii < jl)), 1.0, 0.0)
        sel = (rank < TOPN) & (jl <= qblk) & (jl >= 0) & (jl < NSEL)
        aug = jnp.where(sel, 0.0, jnp.where((jl >= 0) & (jl < NSEL), SEL_OFF, 0.0)).astype(BF16)
        qsel = jnp.where(col4 < 64, qst, jnp.concatenate([aug] * H, axis=0))

        def att_step(j, carry, q_in, k_ref, v_ref, mode):
            m, acc = carry
            rk = _rows(j, TK)
            s = _nt(q_in, k_ref[rk, :])
            if mode is not None:
                kpos = j * TK + col4
                qpos = t0 + row4
                msk = (kpos <= qpos) if mode == "causal" else (kpos > qpos - NSA_WINDOW)
                s = jnp.where(msk, s, NEG_BIG)
            m_new = jnp.maximum(m, jnp.max(s, axis=-1, keepdims=True))
            p = jnp.exp(s - m_new)
            if mode is not None:
                p = jnp.where(msk, p, 0.0)
            acc = jnp.exp(m - m_new) * acc + _nn(p.astype(BF16), v_ref[rk, :])
            return m_new, acc

        init = (jnp.full((H * TQ, 1), NEG_BIG, F32), jnp.zeros((H * TQ, LANES), F32))
        sel_step = functools.partial(att_step, q_in=qsel, k_ref=ksel_ref, v_ref=vsel_ref)
        carry = lax.fori_loop(0, i, functools.partial(sel_step, mode=None), init)
        _, acc_sel = sel_step(i, carry, mode="causal")
        win_step = functools.partial(att_step, q_in=qst, k_ref=kwin_ref, v_ref=vwin_ref)
        carry = lax.fori_loop(jnp.maximum(i - WT, 0), i, functools.partial(win_step, mode="window"), init)
        _, acc_win = win_step(i, carry, mode="causal")

        gates = _sigmoid(sm_ref[rq, :])
        outs = []
        for h in range(H):
            hr = slice(h * TQ, (h + 1) * TQ)
            o_sel = acc_sel[hr, :HEAD_DIM] / acc_sel[hr, HEAD_DIM:HEAD_DIM + 1]
            o_win = acc_win[hr, :HEAD_DIM] / acc_win[hr, HEAD_DIM:HEAD_DIM + 1]
            c0 = GROUP_HEADS + h
            outs.append(gates[:, c0:c0 + 1] * o_cmp[hr, :HEAD_DIM]
                        + gates[:, c0 + H:c0 + H + 1] * o_sel
                        + gates[:, c0 + 2 * H:c0 + 2 * H + 1] * o_win)
        o_ref[rq, :] = jnp.concatenate(outs, axis=1)
        return 0

    lax.fori_loop(0, T // TQ, q_body, 0)


def _rope_tables(pos):
    half = ROT_DIM // 2
    inv_freq = ROPE_THETA ** (-(jnp.arange(half, dtype=F32) * 2.0 / ROT_DIM))
    ang = pos.astype(F32)[:, None] * inv_freq[None, :]
    cos, sin = jnp.cos(ang), jnp.sin(ang)
    n = pos.shape[0]
    c64 = jnp.concatenate([cos, cos, jnp.ones((n, HEAD_DIM - ROT_DIM), F32)], axis=1)
    s64 = jnp.concatenate([-sin, sin, jnp.zeros((n, HEAD_DIM - ROT_DIM), F32)], axis=1)
    return c64, s64


def _nsa(g_ns, g_sm, pe, w1, w2, B, T):
    TQ = LANES
    NC = T // NSA_CMP_STRIDE
    c64, s64 = _rope_tables(jnp.arange(T))
    cosq, sinq = jnp.tile(c64, (1, 2)), jnp.tile(s64, (1, 2))
    cosk = jnp.concatenate([c64, jnp.ones((T, HEAD_DIM), F32)], axis=1)
    sink = jnp.concatenate([s64, jnp.zeros((T, HEAD_DIM), F32)], axis=1)
    cc, sc = _rope_tables(jnp.arange(NC) * NSA_CMP_STRIDE + NSA_CMP_LEN - 1)
    cosc = jnp.concatenate([cc, jnp.ones((NC, HEAD_DIM), F32)], axis=1)
    sinc = jnp.concatenate([sc, jnp.zeros((NC, HEAD_DIM), F32)], axis=1)
    kern = functools.partial(_nsa_kernel, T=T, TQ=TQ)
    full = lambda shape: pl.BlockSpec(shape, lambda b: (0,) * len(shape))
    ns_w = GROUP_W + 3 * LANES
    return pl.pallas_call(
        kern,
        grid=(B,),
        in_specs=[pl.BlockSpec((None, T, ns_w), lambda b: (b, 0, 0)),
                  pl.BlockSpec((None, T, LANES), lambda b: (b, 0, 0)),
                  full((T, LANES)), full((T, LANES)), full((T, LANES)), full((T, LANES)),
                  full((NC, LANES)), full((NC, LANES)),
                  full((NSA_CMP_LEN, LANES)), full((NSA_CMP_LEN * LANES, LANES)), full((LANES, LANES))],
        out_specs=pl.BlockSpec((None, T, GROUP_W), lambda b: (b, 0, 0)),
        out_shape=jax.ShapeDtypeStruct((B, T, GROUP_W), F32),
        scratch_shapes=[pltpu.VMEM((T + 2 * NSA_CMP_STRIDE, LANES), F32),
                        pltpu.VMEM((NC, NSA_CMP_LEN * LANES), BF16),
                        pltpu.VMEM((GROUP_HEADS, T, LANES), BF16),
                        pltpu.VMEM((NC, LANES), BF16), pltpu.VMEM((NC, LANES), BF16)]
        + [pltpu.VMEM((T, LANES), BF16) for _ in range(4)],
        compiler_params=pltpu.CompilerParams(dimension_semantics=("arbitrary",),
                                             vmem_limit_bytes=VMEM_LIMIT),
        name="nsa",
    )(g_ns, g_sm, cosq, sinq, cosk, sink, cosc, sinc, pe, w1, w2)


def _out_kernel(y_hg, y_fx, y_sb, y_ns, gate_ref, x_ref, gn_ref, w_ref, fg_ref, o_ref, *, final):
    bd = jnp.where(_div(_iota((LANES, LANES), 0), HEAD_DIM) == _div(_iota((LANES, LANES), 1), HEAD_DIM),
                   1.0, 0.0).astype(BF16)
    parts = []
    for ref in (y_hg, y_fx, y_sb, y_ns):
        y = ref[...]
        for j in range(GROUP_W // LANES):
            yy = y[:, j * LANES:(j + 1) * LANES]
            hi, lo = _split2(yy * yy)
            ms = (_nn(hi, bd) + _nn(lo, bd)) * (1.0 / HEAD_DIM)
            parts.append(yy * lax.rsqrt(ms + EPS))
    gt = gate_ref[...]
    z = jnp.concatenate(parts, axis=1) * gn_ref[...] * (gt * _sigmoid(gt))
    out = x_ref[...] + _nn(z.astype(BF16), w_ref[...])
    if final:
        out = out * lax.rsqrt(jnp.mean(out * out, axis=-1, keepdims=True) + EPS) * fg_ref[...]
    o_ref[...] = out


def _out_project(ys, gate, x2, gn, w_out, fg, final, tm):
    n = x2.shape[0]
    kern = functools.partial(_out_kernel, final=final)
    row = lambda wd: pl.BlockSpec((tm, wd), lambda i: (i, 0))
    return pl.pallas_call(
        kern,
        grid=(n // tm,),
        in_specs=[row(GROUP_W)] * 4 + [row(D_MODEL), row(D_MODEL),
                                       pl.BlockSpec((1, D_MODEL), lambda i: (0, 0)),
                                       pl.BlockSpec((D_MODEL, D_MODEL), lambda i: (0, 0)),
                                       pl.BlockSpec((1, D_MODEL), lambda i: (0, 0))],
        out_specs=row(D_MODEL),
        out_shape=jax.ShapeDtypeStruct((n, D_MODEL), F32),
        compiler_params=pltpu.CompilerParams(dimension_semantics=("arbitrary",),
                                             vmem_limit_bytes=VMEM_LIMIT),
        name="outproj",
    )(*ys, gate, x2, gn.reshape(1, D_MODEL), w_out, fg.reshape(1, D_MODEL))


def _pack_in_weights(w):
    o = IN_OFFS
    sm = jnp.concatenate([w[:, o[6]:o[7]], w[:, o[17]:o[18]],
                          jnp.zeros((D_MODEL, LANES - GROUP_HEADS - 3 * GROUP_HEADS), w.dtype)], axis=1)
    groups = [w[:, o[0]:o[3]], w[:, o[3]:o[6]], w[:, o[7]:o[10]], w[:, o[10]:o[17]], w[:, o[18]:o[19]], sm]
    return [g.astype(BF16) for g in groups]


def _pack_cmp_weights(pe_k, w1_k, w2_k, pe_v, w1_v, w2_v):
    L, d = NSA_CMP_LEN, HEAD_DIM
    z = jnp.zeros((L, d, d), F32)
    w1 = jnp.concatenate([jnp.concatenate([w1_k.reshape(L, d, d), z], axis=2),
                          jnp.concatenate([z, w1_v.reshape(L, d, d)], axis=2)], axis=1).reshape(L * 2 * d, 2 * d)
    z2 = jnp.zeros((d, d), F32)
    w2 = jnp.concatenate([jnp.concatenate([w2_k, z2], axis=1), jnp.concatenate([z2, w2_v], axis=1)], axis=0)
    return jnp.concatenate([pe_k, pe_v], axis=1), w1.astype(BF16), w2.astype(BF16)


def kernel(x, norm_g, w_in, hgrn_lb_logits, fox_fb, nsa_cmp_pe_k, nsa_cmp_w1_k, nsa_cmp_w2_k,
           nsa_cmp_pe_v, nsa_cmp_w1_v, nsa_cmp_w2_v, out_norm_g, w_out, final_norm_g):
    B, T, D = x.shape
    depth = w_in.shape[0]
    x2 = x.reshape(B * T, D)
    for l in range(depth):
        g_hg, g_fx, g_sb, g_ns, g_gate, g_sm = _project(x2, norm_g[l], _pack_in_weights(w_in[l]), 512)
        g_sm3 = g_sm.reshape(B, T, LANES)
        fb = jnp.concatenate([fox_fb[l], jnp.zeros((LANES - GROUP_HEADS,), F32)]).reshape(1, LANES)
        pe, w1, w2 = _pack_cmp_weights(nsa_cmp_pe_k[l], nsa_cmp_w1_k[l], nsa_cmp_w2_k[l],
                                       nsa_cmp_pe_v[l], nsa_cmp_w1_v[l], nsa_cmp_w2_v[l])
        y_hg = _hgrn2(g_hg.reshape(B, T, -1), hgrn_lb_logits, l, B, T)
        y_fx = _fox(g_fx.reshape(B, T, -1), g_sm3, fb, B, T)
        y_sb = _stick_breaking(g_sb.reshape(B, T, -1), B, T)
        y_ns = _nsa(g_ns.reshape(B, T, -1), g_sm3, pe, w1, w2, B, T)
        ys = [y.reshape(B * T, GROUP_W) for y in (y_hg, y_fx, y_sb, y_ns)]
        x2 = _out_project(ys, g_gate, x2, out_norm_g[l], w_out[l].astype(BF16), final_norm_g,
                          l == depth - 1, 512)
    return x2.reshape(B, T, D)
```

```python
import functools

import numpy as np
import jax
import jax.numpy as jnp
from jax import lax
from jax.experimental import pallas as pl
from jax.experimental.pallas import tpu as pltpu

F32 = jnp.float32
BF16 = jnp.bfloat16

D_MODEL = 1024
HEAD_DIM = 64
GROUP_W = 256
GROUP_HEADS = 4
ATTN_SCALE = HEAD_DIM ** -0.5
LOG2E = 1.4426950408889634
SCALE_LOG2 = ATTN_SCALE * LOG2E
ROPE_THETA = 500000.0
ROT_DIM = HEAD_DIM // 4
HG_CHUNK = 64
NSA_CMP_LEN = 32
NSA_CMP_STRIDE = 16
NSA_SEL_BLOCK = 64
NSA_TOP_N = 16
NSA_WINDOW = 512
NSA_FORCE = 1.0e4
NEG_BIG = -1.0e30
SEL_OFF = -(2.0 ** 100)
EPS = 1e-6

LANES = 128
VMEM_LIMIT = 56 * 1024 * 1024

IN_WIDTHS = (GROUP_W,) * 3 + (GROUP_W,) * 3 + (GROUP_HEADS,) + (GROUP_W,) * 3 + (GROUP_W,) \
    + (HEAD_DIM,) * 6 + (3 * GROUP_HEADS,) + (D_MODEL,)
IN_OFFS = np.concatenate([[0], np.cumsum(IN_WIDTHS)]).tolist()


def _nn(a, b):
    return jnp.dot(a, b, preferred_element_type=F32)


def _nt(a, b):
    return lax.dot_general(a, b, (((1,), (1,)), ((), ())), preferred_element_type=F32)


def _tn(a, b):
    return lax.dot_general(a, b, (((0,), (0,)), ((), ())), preferred_element_type=F32)


def _iota(shape, dim):
    return lax.broadcasted_iota(jnp.int32, shape, dim)


def _split2(x):
    hi = x.astype(BF16)
    lo = (x - hi.astype(F32)).astype(BF16)
    return hi, lo


def _split3(x):
    p1 = x.astype(BF16).astype(F32)
    r = x - p1
    p2 = r.astype(BF16).astype(F32)
    p3 = (r - p2).astype(BF16).astype(F32)
    return p1, p2, p3


def _sigmoid(x):
    return 1.0 / (1.0 + jnp.exp(-x))


def _log_sigmoid(x):
    return jnp.minimum(x, 0.0) - jnp.log(1.0 + jnp.exp(-jnp.abs(x)))


def _div(x, n):
    assert n & (n - 1) == 0
    return x >> (n.bit_length() - 1)


def _rows(i, n):
    return pl.ds(pl.multiple_of(i * n, n), n)


def _head_to_slot(x, h):
    return x if h % 2 == 0 else pltpu.roll(x, HEAD_DIM, 1)


def _proj_kernel(x_ref, g_ref, w_hg, w_fx, w_sb, w_ns, w_gate, w_sm,
                 o_hg, o_fx, o_sb, o_ns, o_gate, o_sm):
    x = x_ref[...]
    h = x * lax.rsqrt(jnp.mean(x * x, axis=-1, keepdims=True) + EPS) * g_ref[...]
    hb = h.astype(BF16)
    for w, o in ((w_hg, o_hg), (w_fx, o_fx), (w_sb, o_sb), (w_ns, o_ns), (w_gate, o_gate), (w_sm, o_sm)):
        o[...] = _nn(hb, w[...])


def _project(x2, norm_g, ws, tm):
    n = x2.shape[0]
    widths = [w.shape[1] for w in ws]
    return pl.pallas_call(
        _proj_kernel,
        grid=(n // tm,),
        in_specs=[pl.BlockSpec((tm, D_MODEL), lambda i: (i, 0)),
                  pl.BlockSpec((1, D_MODEL), lambda i: (0, 0))]
        + [pl.BlockSpec((D_MODEL, wd), lambda i: (0, 0)) for wd in widths],
        out_specs=[pl.BlockSpec((tm, wd), lambda i: (i, 0)) for wd in widths],
        out_shape=[jax.ShapeDtypeStruct((n, wd), F32) for wd in widths],
        compiler_params=pltpu.CompilerParams(dimension_semantics=("arbitrary",),
                                             vmem_limit_bytes=VMEM_LIMIT),
        name="proj",
    )(x2, norm_g.reshape(1, D_MODEL), *ws)


def _fox_kernel(g_ref, sm_ref, fb_ref, o_ref, cum_ref, qa_ref, ka_ref, va_ref, m_ref, acc_ref, *, T, TQ, TK):
    H = GROUP_HEADS
    RB = 256
    NB = T // LANES
    tri = jnp.where(_iota((LANES, LANES), 0) >= _iota((LANES, LANES), 1), 1.0, 0.0).astype(BF16)
    fb = fb_ref[...]

    local = []
    for i in range(NB):
        p1, p2, p3 = _split3(_log_sigmoid(sm_ref[i * LANES:(i + 1) * LANES, :] + fb))
        local.append(_nn(tri, p1.astype(BF16)) + _nn(tri, p2.astype(BF16)) + _nn(tri, p3.astype(BF16)))
    carry = jnp.zeros((1, LANES), F32)
    for i in range(NB):
        c = local[i] + carry
        cum_ref[i * LANES:(i + 1) * LANES, :] = c
        carry = c[LANES - 1:LANES, :]

    lane = _iota((RB, LANES), 1)
    upper = lane >= HEAD_DIM

    def build_body(i, _):
        r = _rows(i, RB)
        cum = cum_ref[r, :]
        for h in range(H):
            j = h // 2
            own = upper if h % 2 else ~upper
            a0 = 0 if h % 2 else HEAD_DIM
            q = g_ref[r, j * LANES:(j + 1) * LANES] * SCALE_LOG2
            k = g_ref[r, GROUP_W + j * LANES:GROUP_W + (j + 1) * LANES]
            v = g_ref[r, 2 * GROUP_W + j * LANES:2 * GROUP_W + (j + 1) * LANES]
            qa = jnp.where(own, q, 0.0)
            ka = jnp.where(own, k, 0.0)
            for p, c in enumerate(_split3(jnp.broadcast_to(cum[:, h:h + 1], (RB, LANES)) * LOG2E)):
                qa = jnp.where(lane == a0 + p, c, qa)
                qa = jnp.where(lane == a0 + 3 + p, 1.0, qa)
                ka = jnp.where(lane == a0 + p, 1.0, ka)
                ka = jnp.where(lane == a0 + 3 + p, -c, ka)
            qa_ref[h, r, :] = qa.astype(BF16)
            ka_ref[h, r, :] = ka.astype(BF16)
            va_ref[h, r, :] = jnp.where(own, v, 1.0).astype(BF16)
        return 0

    lax.fori_loop(0, T // RB, build_body, 0)

    NSUB = TQ // TK
    col = _iota((TQ, TK), 1)
    row = _iota((TQ, TK), 0)
    upper_q = _iota((TQ, LANES), 1) >= HEAD_DIM

    def q_body(i, _):
        rq = _rows(i, TQ)
        qas = [qa_ref[h, rq, :] for h in range(H)]
        for h in range(H):
            m_ref[h] = jnp.full((TQ, LANES), NEG_BIG, F32)
            acc_ref[h] = jnp.zeros((TQ, LANES), F32)

        def k_step(j, masked):
            rk = _rows(j, TK)
            for h in range(H):
                s = _nt(qas[h], ka_ref[h, rk, :])
                if masked:
                    s = jnp.where(j * TK + col <= i * TQ + row, s, NEG_BIG)
                m = m_ref[h]
                m_new = jnp.maximum(m, jnp.max(s, axis=-1, keepdims=True))
                p = jnp.exp2(jnp.concatenate([s[:, c0:c0 + LANES] - m_new for c0 in range(0, TK, LANES)], axis=1))
                acc_ref[h] = jnp.exp2(m - m_new) * acc_ref[h] + _nn(p.astype(BF16), va_ref[h, rk, :])
                m_ref[h] = m_new

        def k_body(jj, _):
            for d in range(NSUB):
                k_step(jj * NSUB + d, False)
            return 0

        lax.fori_loop(0, i, k_body, 0)
        for d in range(NSUB):
            k_step(i * NSUB + d, True)
        for j in range(H // 2):
            ae, ao = acc_ref[2 * j], acc_ref[2 * j + 1]
            den = pltpu.roll(jnp.where(upper_q, ae, ao), HEAD_DIM, 1)
            o_ref[rq, j * LANES:(j + 1) * LANES] = jnp.where(upper_q, ao, ae) / den
        return 0

    lax.fori_loop(0, T // TQ, q_body, 0)


def _fox(g_fx, g_sm, fb, B, T):
    TQ, TK = 512, 256
    kern = functools.partial(_fox_kernel, T=T, TQ=TQ, TK=TK)
    return pl.pallas_call(
        kern,
        grid=(B,),
        in_specs=[pl.BlockSpec((None, T, 3 * GROUP_W), lambda b: (b, 0, 0)),
                  pl.BlockSpec((None, T, LANES), lambda b: (b, 0, 0)),
                  pl.BlockSpec((1, LANES), lambda b: (0, 0))],
        out_specs=pl.BlockSpec((None, T, GROUP_W), lambda b: (b, 0, 0)),
        out_shape=jax.ShapeDtypeStruct((B, T, GROUP_W), F32),
        scratch_shapes=[pltpu.VMEM((T, LANES), F32)]
        + [pltpu.VMEM((GROUP_HEADS, T, LANES), BF16) for _ in range(3)]
        + [pltpu.VMEM((GROUP_HEADS, TQ, LANES), F32) for _ in range(2)],
        compiler_params=pltpu.CompilerParams(dimension_semantics=("arbitrary",),
                                             vmem_limit_bytes=VMEM_LIMIT),
        name="fox",
    )(g_fx, g_sm, fb)


def _sb_kernel(g_ref, o_ref, qa_ref, ka_ref, va_ref, c_ref, acc_ref, *, T, TQ, TK):
    H = GROUP_HEADS
    RB = 256
    SUB = LANES
    NSUB = TQ // TK
    upper = _iota((RB, LANES), 1) >= HEAD_DIM

    def build_body(i, _):
        r = _rows(i, RB)
        for h in range(H):
            j = h // 2
            own = upper if h % 2 else ~upper
            q = g_ref[r, j * LANES:(j + 1) * LANES] * ATTN_SCALE
            k = g_ref[r, GROUP_W + j * LANES:GROUP_W + (j + 1) * LANES]
            v = g_ref[r, 2 * GROUP_W + j * LANES:2 * GROUP_W + (j + 1) * LANES]
            qa_ref[h, r, :] = jnp.where(own, q, 0.0).astype(BF16)
            ka_ref[h, r, :] = jnp.where(own, k, 0.0).astype(BF16)
            va_ref[h, r, :] = jnp.where(own, v, 0.0).astype(BF16)
        return 0

    lax.fori_loop(0, T // RB, build_body, 0)

    wr = _iota((2 * SUB, 2 * SUB), 0) & (SUB - 1)
    wc = _iota((2 * SUB, 2 * SUB), 1)
    suf_w = jnp.where((wc >= SUB) | (wr > wc), 1.0, 0.0).astype(BF16)
    row = _iota((TQ, TK), 0)
    col = _iota((TQ, TK), 1)

    def q_body(i, _):
        rq = _rows(i, TQ)
        qas = [qa_ref[h, rq, :] for h in range(H)]
        for h in range(H):
            c_ref[h] = jnp.zeros((TQ, LANES), F32)
        for j in range(H // 2):
            acc_ref[j] = jnp.zeros((TQ, LANES), F32)

        def k_step(j, masked):
            rk = _rows(j, TK)
            for h in range(H):
                z = _nt(qas[h], ka_ref[h, rk, :])
                ls = jnp.minimum(z, 0.0) - jnp.log(1.0 + jnp.exp(-jnp.abs(z)))
                lom = ls - z
                if masked:
                    msk = (j * TK + col) < (i * TQ + row)
                    lom = jnp.where(msk, lom, 0.0)
                c = c_ref[h]
                parts = []
                for sb in reversed(range(TK // SUB)):
                    hi, lo = _split2(lom[:, sb * SUB:(sb + 1) * SUB])
                    cs = _nn(jnp.concatenate([hi, lo], axis=1), suf_w)
                    parts.append(ls[:, sb * SUB:(sb + 1) * SUB] + cs[:, :SUB] + c)
                    c = c + cs[:, SUB:]
                c_ref[h] = c
                a = jnp.exp(jnp.concatenate(parts[::-1], axis=1))
                if masked:
                    a = jnp.where(msk, a, 0.0)
                acc_ref[h // 2] += _nn(a.astype(BF16), va_ref[h, rk, :])

        for d in range(NSUB):
            k_step(i * NSUB + (NSUB - 1 - d), True)

        def k_body(jj, _):
            for d in range(NSUB):
                k_step((i - jj) * NSUB - 1 - d, False)
            return 0

        lax.fori_loop(0, i, k_body, 0)
        for j in range(H // 2):
            o_ref[rq, j * LANES:(j + 1) * LANES] = acc_ref[j]
        return 0

    lax.fori_loop(0, T // TQ, q_body, 0)


def _stick_breaking(g_sb, B, T):
    TQ, TK = 512, 256
    kern = functools.partial(_sb_kernel, T=T, TQ=TQ, TK=TK)
    return pl.pallas_call(
        kern,
        grid=(B,),
        in_specs=[pl.BlockSpec((None, T, 3 * GROUP_W), lambda b: (b, 0, 0))],
        out_specs=pl.BlockSpec((None, T, GROUP_W), lambda b: (b, 0, 0)),
        out_shape=jax.ShapeDtypeStruct((B, T, GROUP_W), F32),
        scratch_shapes=[pltpu.VMEM((GROUP_HEADS, T, LANES), BF16) for _ in range(3)]
        + [pltpu.VMEM((GROUP_HEADS, TQ, LANES), F32), pltpu.VMEM((GROUP_HEADS // 2, TQ, LANES), F32)],
        compiler_params=pltpu.CompilerParams(dimension_semantics=("arbitrary",),
                                             vmem_limit_bytes=VMEM_LIMIT),
        name="stickbreak",
    )(g_sb)


def _hgrn_decay_matrix(C):
    nl = C.bit_length() - 1
    t = _iota((C, C), 0)
    u = _iota((C, C), 1)
    blocks = [u <= t, u > t]
    q_blocks, k_blocks = [], []
    for l in range(nl):
        m = C >> (l + 1)
        bnd = (t & ~(2 * m - 1)) + (m - 1)
        q_blocks.append((u > bnd) & (u <= t))
        k_blocks.append((u > t) & (u <= bnd))
    d = jnp.concatenate([jnp.where(b, 1.0, 0.0) for b in blocks + q_blocks + k_blocks], axis=0).astype(BF16)
    return jnp.concatenate([d, d, d], axis=1)


def _hgrn_kernel(g_ref, lb_ref, o_ref, st_ref, *, T, C, layer):
    H = GROUP_HEADS
    nl = C.bit_length() - 1
    lg = lb_ref[...]
    e = jnp.exp(lg - jnp.max(lg, axis=0, keepdims=True))
    sm = e / jnp.sum(e, axis=0, keepdims=True)
    lb = jnp.sum(sm[0:layer + 1, :], axis=0, keepdims=True) - sm[0:1, :]

    dmat = _hgrn_decay_matrix(C)
    lane_head = _div(_iota((C, GROUP_W), 1), HEAD_DIM)
    lane_head4 = _div(_iota((H * C, GROUP_W), 1), HEAD_DIM)
    row_head4 = _div(_iota((H * C, GROUP_W), 0), C)
    head_sel = lane_head4 == row_head4
    tq_c = _iota((C, GROUP_W), 0)
    t4 = _iota((H * C, C), 0) & (C - 1)
    s4 = _iota((H * C, C), 1)
    bd_mask = _div(_iota((GROUP_W, GROUP_W), 0), HEAD_DIM) == _div(_iota((GROUP_W, GROUP_W), 1), HEAD_DIM)
    st_ref[...] = jnp.zeros((GROUP_W, GROUP_W), F32)

    def chunk(ci, _):
        r = _rows(ci, C)
        qraw = g_ref[r, 0:GROUP_W]
        f = g_ref[r, GROUP_W:2 * GROUP_W]
        v = g_ref[r, 2 * GROUP_W:3 * GROUP_W].astype(BF16)
        q = qraw * _sigmoid(qraw)
        g = jnp.log(lb + (1.0 - lb) * _sigmoid(f))
        k = (1.0 - lb) * _sigmoid(-f)
        g1, g2, g3 = _split3(g)
        ex = _nn(dmat, jnp.concatenate([g1.astype(BF16), g2.astype(BF16), g3.astype(BF16)], axis=0))
        b = ex[0:C]
        st = st_ref[...]
        o_inter = _nt((q * jnp.exp(b)).astype(BF16), st.astype(BF16))

        a = jnp.zeros((H * C, C), F32)
        for l in range(nl + 1):
            if l < nl:
                m = C >> (l + 1)
                dq = ex[(2 + l) * C:(3 + l) * C]
                dk = ex[(2 + nl + l) * C:(3 + nl + l) * C]
                qs = q * jnp.where((tq_c & m) != 0, jnp.exp(dq), 0.0)
                ks = k * jnp.where((tq_c & m) == 0, jnp.exp(dk), 0.0)
                pair = (t4 & ~(2 * m - 1)) == (s4 & ~(2 * m - 1))
            else:
                qs, ks = q, k
                pair = t4 == s4
            qsb = qs.astype(BF16)
            qst = jnp.where(head_sel, jnp.concatenate([qsb] * H, axis=0), jnp.zeros((), BF16))
            a = a + jnp.where(pair, _nt(qst, ks.astype(BF16)), 0.0)
        rr = _nn(a.astype(BF16), v)
        o_intra = rr[0:C]
        for h in range(1, H):
            o_intra = jnp.where(lane_head == h, rr[h * C:(h + 1) * C], o_intra)
        o_ref[r, :] = o_inter + o_intra

        kd = (k * jnp.exp(ex[C:2 * C])).astype(BF16)
        upd = _tn(v, kd)
        st_ref[...] = st * jnp.exp(b[C - 1:C, :]) + jnp.where(bd_mask, upd, 0.0)
        return 0

    lax.fori_loop(0, T // C, chunk, 0)


def _hgrn2(g_hg, lb_logits, layer, B, T):
    depth = lb_logits.shape[0]
    kern = functools.partial(_hgrn_kernel, T=T, C=HG_CHUNK, layer=layer)
    return pl.pallas_call(
        kern,
        grid=(B,),
        in_specs=[pl.BlockSpec((None, T, 3 * GROUP_W), lambda b: (b, 0, 0)),
                  pl.BlockSpec((depth, GROUP_W), lambda b: (0, 0))],
        out_specs=pl.BlockSpec((None, T, GROUP_W), lambda b: (b, 0, 0)),
        out_shape=jax.ShapeDtypeStruct((B, T, GROUP_W), F32),
        scratch_shapes=[pltpu.VMEM((GROUP_W, GROUP_W), F32)],
        compiler_params=pltpu.CompilerParams(dimension_semantics=("arbitrary",),
                                             vmem_limit_bytes=VMEM_LIMIT),
        name="hgrn2",
    )(g_hg, lb_logits)


def _rope(x, cos, sin, lane):
    swapped = jnp.where((lane & (HEAD_DIM - 1)) < ROT_DIM // 2,
                        pltpu.roll(x, LANES - ROT_DIM // 2, 1), pltpu.roll(x, ROT_DIM // 2, 1))
    return x * cos + swapped * sin


def _nsa_kernel(g_ref, sm_ref, cosq_ref, sinq_ref, cosk_ref, sink_ref, cosc_ref, sinc_ref,
                pe_ref, w1_ref, w2_ref, o_ref,
                cv_ref, ablk_ref, qa_ref, kcmp_ref, vcmp_ref, ksel_ref, vsel_ref, kwin_ref, vwin_ref,
                m_ref, acc_ref, score_ref, *, T, TQ, TK):
    H = GROUP_HEADS
    RB = 256
    NC = T // NSA_CMP_STRIDE
    NSEL = T // NSA_SEL_BLOCK
    TOPN = min(NSA_TOP_N, NSEL)
    WT = NSA_WINDOW // TK
    lane = _iota((RB, LANES), 1)
    grow = _iota((RB, LANES), 0)

    def build_body(i, _):
        r = _rows(i, RB)
        cosq, sinq = cosq_ref[r, :], sinq_ref[r, :]
        cosk, sink = cosk_ref[r, :], sink_ref[r, :]
        for j in range(2):
            xr = _rope(g_ref[r, j * LANES:(j + 1) * LANES], cosq, sinq, lane) * SCALE_LOG2
            for h in (2 * j, 2 * j + 1):
                qa_ref[h, r, :] = jnp.where(lane < 64, _head_to_slot(xr, h), 0.0).astype(BF16)
        cv_ref[r, :] = g_ref[r, GROUP_W:GROUP_W + LANES]
        ksvs = g_ref[r, GROUP_W + LANES:GROUP_W + 2 * LANES]
        kwvw = g_ref[r, GROUP_W + 2 * LANES:GROUP_W + 3 * LANES]
        blk = _div(i * RB + grow, NSA_SEL_BLOCK)
        onehot = jnp.where((lane >= 64) & (lane - 64 == blk), 1.0, 0.0)
        ksel_ref[r, :] = jnp.where(lane < 64, _rope(ksvs, cosk, sink, lane), onehot).astype(BF16)
        vsel_ref[r, :] = jnp.where(lane < 64, pltpu.roll(ksvs, 64, 1), 1.0).astype(BF16)
        kwin_ref[r, :] = jnp.where(lane < 64, _rope(kwvw, cosk, sink, lane), 0.0).astype(BF16)
        vwin_ref[r, :] = jnp.where(lane < 64, pltpu.roll(kwvw, 64, 1), 1.0).astype(BF16)
        return 0

    lax.fori_loop(0, T // RB, build_body, 0)
    cv_ref[T:T + 2 * NSA_CMP_STRIDE, :] = jnp.zeros((2 * NSA_CMP_STRIDE, LANES), F32)

    for l in range(NSA_CMP_LEN):
        blk_l = cv_ref[pl.ds(l, NC, stride=NSA_CMP_STRIDE), :] + pe_ref[l:l + 1, :]
        ablk_ref[:, l * LANES:(l + 1) * LANES] = blk_l.astype(BF16)
    hid = _nn(ablk_ref[...], w1_ref[...])
    hid = hid * _sigmoid(hid)
    kv = _nn(hid.astype(BF16), w2_ref[...])
    lane_c = _iota((NC, LANES), 1)
    kcmp_ref[...] = jnp.where(lane_c < 64, _rope(kv, cosc_ref[...], sinc_ref[...], lane_c), 0.0).astype(BF16)
    vcmp_ref[...] = jnp.where(lane_c < 64, pltpu.roll(kv, 64, 1), 0.0).astype(BF16)

    on = _iota((NSEL, 2 * NC), 1) & (NC - 1)
    oj = _iota((NSEL, 2 * NC), 0)
    ovt = jnp.clip(jnp.minimum(on * NSA_CMP_STRIDE + NSA_CMP_LEN, oj * NSA_SEL_BLOCK + NSA_SEL_BLOCK)
                   - jnp.maximum(on * NSA_CMP_STRIDE, oj * NSA_SEL_BLOCK), 0, None).astype(F32) / NSA_CMP_LEN
    ovt = jnp.where(on < NC - 1, ovt, 0.0).astype(BF16)
    jrow = _iota((NSEL, TQ), 0)
    tcol = _iota((NSEL, TQ), 1)

    gr = _iota((2 * LANES, 3 * GROUP_W), 0) & (LANES - 1)
    gc = _iota((2 * LANES, 3 * GROUP_W), 1)
    gate_w = jnp.where(gr == GROUP_HEADS + _div(gc, GROUP_W) * GROUP_HEADS + _div(gc & (GROUP_W - 1), HEAD_DIM),
                       1.0, 0.0).astype(BF16)

    row4 = _iota((H * TQ, TK), 0) & (TQ - 1)
    col4 = _iota((H * TQ, TK), 1)
    lane4 = _iota((H * TQ, LANES), 1)
    lane_q = _iota((TQ, LANES), 1)
    row_q = _iota((TQ, LANES), 0)
    cmp_end = _iota((H * TQ, NC), 1) * NSA_CMP_STRIDE + (NSA_CMP_LEN - 1)
    rowc = _iota((H * TQ, NC), 0) & (TQ - 1)

    def q_body(i, _):
        rq = _rows(i, TQ)
        t0 = i * TQ
        qst = jnp.concatenate([qa_ref[h, rq, :] for h in range(H)], axis=0)

        valid = cmp_end <= (t0 + rowc)
        s = jnp.where(valid, _nt(qst, kcmp_ref[...]), NEG_BIG)
        p = jnp.where(valid, jnp.exp2(s - jnp.max(s, axis=-1, keepdims=True)), 0.0)
        den = jnp.sum(p, axis=-1, keepdims=True)
        p = p / jnp.where(den > 0, den, 1.0)
        o_cmp = _nn(p.astype(BF16), vcmp_ref[...])
        psum = p[0:TQ]
        for h in range(1, H):
            psum = psum + p[h * TQ:(h + 1) * TQ]
        hi, lo = _split2(psum)
        imp = _nt(ovt, jnp.concatenate([hi, lo], axis=1))

        qblk = _div(t0 + tcol, NSA_SEL_BLOCK)
        forced = (jrow == 0) | (jrow == qblk) | (jrow == qblk - 1)
        score = jnp.where(forced, NSA_FORCE, imp)
        score_ref[...] = jnp.where(jrow <= qblk, score, -NSA_FORCE)
        score = score_ref[...]
        rank = jnp.zeros((NSEL, TQ), F32)
        for ii in range(NSEL):
            ci = score_ref[ii:ii + 1, :]
            rank = rank + jnp.where((ci > score) | ((ci == score) & (ii < jrow)), 1.0, 0.0)
        sel = (rank < TOPN) & (jrow <= qblk)
        bias = jnp.where(sel, 0.0, SEL_OFF)
        aug = jnp.concatenate([jnp.zeros((HEAD_DIM, TQ), F32), bias,
                               jnp.zeros((LANES - HEAD_DIM - NSEL, TQ), F32)], axis=0).T.astype(BF16)
        qsel = jnp.where(lane4 < 64, qst, jnp.concatenate([aug] * H, axis=0))

        def att_step(j, _, slot, q_in, k_ref, v_ref, mode):
            rk = _rows(j, TK)
            s = _nt(q_in, k_ref[rk, :])
            if mode is not None:
                kpos = j * TK + col4
                qpos = t0 + row4
                msk = (kpos <= qpos) if mode == "causal" else (kpos > qpos - NSA_WINDOW)
                s = jnp.where(msk, s, NEG_BIG)
            m = m_ref[slot]
            m_new = jnp.maximum(m, jnp.max(s, axis=-1, keepdims=True))
            p = jnp.exp2(jnp.concatenate([s[:, c0:c0 + LANES] - m_new for c0 in range(0, TK, LANES)], axis=1))
            if mode is not None:
                p = jnp.where(msk, p, 0.0)
            acc_ref[slot] = jnp.exp2(m - m_new) * acc_ref[slot] + _nn(p.astype(BF16), v_ref[rk, :])
            m_ref[slot] = m_new
            return 0

        for slot in range(2):
            m_ref[slot] = jnp.full((H * TQ, LANES), NEG_BIG, F32)
            acc_ref[slot] = jnp.zeros((H * TQ, LANES), F32)
        sel_step = functools.partial(att_step, slot=0, q_in=qsel, k_ref=ksel_ref, v_ref=vsel_ref)
        lax.fori_loop(0, i, functools.partial(sel_step, mode=None), 0)
        sel_step(i, 0, mode="causal")
        win_step = functools.partial(att_step, slot=1, q_in=qst, k_ref=kwin_ref, v_ref=vwin_ref)
        lax.fori_loop(jnp.maximum(i - WT, 0), i, functools.partial(win_step, mode="window"), 0)
        win_step(i, 0, mode="causal")

        hi, lo = _split2(_sigmoid(sm_ref[rq, :]))
        gmap = _nn(jnp.concatenate([hi, lo], axis=1), gate_w)
        low = lane_q < 64
        for j in range(H // 2):
            re = slice(2 * j * TQ, (2 * j + 1) * TQ)
            ro = slice((2 * j + 1) * TQ, (2 * j + 2) * TQ)
            out = gmap[:, j * LANES:(j + 1) * LANES] * jnp.where(low, o_cmp[re], pltpu.roll(o_cmp[ro], 64, 1))
            for slot in range(2):
                ae, ao = acc_ref[slot, re, :], acc_ref[slot, ro, :]
                num = jnp.where(low, ae, pltpu.roll(ao, 64, 1))
                den = jnp.where(low, pltpu.roll(ae, 64, 1), ao)
                c0 = (slot + 1) * GROUP_W + j * LANES
                out = out + gmap[:, c0:c0 + LANES] * (num / den)
            o_ref[rq, j * LANES:(j + 1) * LANES] = out
        return 0

    lax.fori_loop(0, T // TQ, q_body, 0)


def _rope_tables(pos):
    half = ROT_DIM // 2
    inv_freq = ROPE_THETA ** (-(jnp.arange(half, dtype=F32) * 2.0 / ROT_DIM))
    ang = pos.astype(F32)[:, None] * inv_freq[None, :]
    cos, sin = jnp.cos(ang), jnp.sin(ang)
    n = pos.shape[0]
    c64 = jnp.concatenate([cos, cos, jnp.ones((n, HEAD_DIM - ROT_DIM), F32)], axis=1)
    s64 = jnp.concatenate([-sin, sin, jnp.zeros((n, HEAD_DIM - ROT_DIM), F32)], axis=1)
    return c64, s64


def _nsa(g_ns, g_sm, pe, w1, w2, B, T):
    TQ = TK = 256
    NC = T // NSA_CMP_STRIDE
    c64, s64 = _rope_tables(jnp.arange(T))
    cosq, sinq = jnp.tile(c64, (1, 2)), jnp.tile(s64, (1, 2))
    cosk = jnp.concatenate([c64, jnp.ones((T, HEAD_DIM), F32)], axis=1)
    sink = jnp.concatenate([s64, jnp.zeros((T, HEAD_DIM), F32)], axis=1)
    cc, sc = _rope_tables(jnp.arange(NC) * NSA_CMP_STRIDE + NSA_CMP_LEN - 1)
    cosc = jnp.concatenate([cc, jnp.ones((NC, HEAD_DIM), F32)], axis=1)
    sinc = jnp.concatenate([sc, jnp.zeros((NC, HEAD_DIM), F32)], axis=1)
    kern = functools.partial(_nsa_kernel, T=T, TQ=TQ, TK=TK)
    full = lambda shape: pl.BlockSpec(shape, lambda b: (0,) * len(shape))
    ns_w = GROUP_W + 3 * LANES
    return pl.pallas_call(
        kern,
        grid=(B,),
        in_specs=[pl.BlockSpec((None, T, ns_w), lambda b: (b, 0, 0)),
                  pl.BlockSpec((None, T, LANES), lambda b: (b, 0, 0)),
                  full((T, LANES)), full((T, LANES)), full((T, LANES)), full((T, LANES)),
                  full((NC, LANES)), full((NC, LANES)),
                  full((NSA_CMP_LEN, LANES)), full((NSA_CMP_LEN * LANES, LANES)), full((LANES, LANES))],
        out_specs=pl.BlockSpec((None, T, GROUP_W), lambda b: (b, 0, 0)),
        out_shape=jax.ShapeDtypeStruct((B, T, GROUP_W), F32),
        scratch_shapes=[pltpu.VMEM((T + 2 * NSA_CMP_STRIDE, LANES), F32),
                        pltpu.VMEM((NC, NSA_CMP_LEN * LANES), BF16),
                        pltpu.VMEM((GROUP_HEADS, T, LANES), BF16),
                        pltpu.VMEM((NC, LANES), BF16), pltpu.VMEM((NC, LANES), BF16)]
        + [pltpu.VMEM((T, LANES), BF16) for _ in range(4)]
        + [pltpu.VMEM((2, GROUP_HEADS * TQ, LANES), F32) for _ in range(2)]
        + [pltpu.VMEM((T // NSA_SEL_BLOCK, TQ), F32)],
        compiler_params=pltpu.CompilerParams(dimension_semantics=("arbitrary",),
                                             vmem_limit_bytes=VMEM_LIMIT),
        name="nsa",
    )(g_ns, g_sm, cosq, sinq, cosk, sink, cosc, sinc, pe, w1, w2)


def _out_kernel(y_hg, y_fx, y_sb, y_ns, gate_ref, x_ref, gn_ref, w_ref, fg_ref, o_ref, *, final):
    bd = jnp.where(_div(_iota((LANES, LANES), 0), HEAD_DIM) == _div(_iota((LANES, LANES), 1), HEAD_DIM),
                   1.0, 0.0).astype(BF16)
    parts = []
    for ref in (y_hg, y_fx, y_sb, y_ns):
        y = ref[...]
        for j in range(GROUP_W // LANES):
            yy = y[:, j * LANES:(j + 1) * LANES]
            hi, lo = _split2(yy * yy)
            ms = (_nn(hi, bd) + _nn(lo, bd)) * (1.0 / HEAD_DIM)
            parts.append(yy * lax.rsqrt(ms + EPS))
    gt = gate_ref[...]
    z = jnp.concatenate(parts, axis=1) * gn_ref[...] * (gt * _sigmoid(gt))
    out = x_ref[...] + _nn(z.astype(BF16), w_ref[...])
    if final:
        out = out * lax.rsqrt(jnp.mean(out * out, axis=-1, keepdims=True) + EPS) * fg_ref[...]
    o_ref[...] = out


def _out_project(ys, gate, x2, gn, w_out, fg, final, tm):
    n = x2.shape[0]
    kern = functools.partial(_out_kernel, final=final)
    row = lambda wd: pl.BlockSpec((tm, wd), lambda i: (i, 0))
    return pl.pallas_call(
        kern,
        grid=(n // tm,),
        in_specs=[row(GROUP_W)] * 4 + [row(D_MODEL), row(D_MODEL),
                                       pl.BlockSpec((1, D_MODEL), lambda i: (0, 0)),
                                       pl.BlockSpec((D_MODEL, D_MODEL), lambda i: (0, 0)),
                                       pl.BlockSpec((1, D_MODEL), lambda i: (0, 0))],
        out_specs=row(D_MODEL),
        out_shape=jax.ShapeDtypeStruct((n, D_MODEL), F32),
        compiler_params=pltpu.CompilerParams(dimension_semantics=("arbitrary",),
                                             vmem_limit_bytes=VMEM_LIMIT),
        name="outproj",
    )(*ys, gate, x2, gn.reshape(1, D_MODEL), w_out, fg.reshape(1, D_MODEL))


def _pack_in_weights(w):
    o = IN_OFFS
    sm = jnp.concatenate([w[:, o[6]:o[7]], w[:, o[17]:o[18]],
                          jnp.zeros((D_MODEL, LANES - GROUP_HEADS - 3 * GROUP_HEADS), w.dtype)], axis=1)
    groups = [w[:, o[0]:o[3]], w[:, o[3]:o[6]], w[:, o[7]:o[10]], w[:, o[10]:o[17]], w[:, o[18]:o[19]], sm]
    return [g.astype(BF16) for g in groups]


def _pack_cmp_weights(pe_k, w1_k, w2_k, pe_v, w1_v, w2_v):
    L, d = NSA_CMP_LEN, HEAD_DIM
    z = jnp.zeros((L, d, d), F32)
    w1 = jnp.concatenate([jnp.concatenate([w1_k.reshape(L, d, d), z], axis=2),
                          jnp.concatenate([z, w1_v.reshape(L, d, d)], axis=2)], axis=1).reshape(L * 2 * d, 2 * d)
    z2 = jnp.zeros((d, d), F32)
    w2 = jnp.concatenate([jnp.concatenate([w2_k, z2], axis=1), jnp.concatenate([z2, w2_v], axis=1)], axis=0)
    return jnp.concatenate([pe_k, pe_v], axis=1), w1.astype(BF16), w2.astype(BF16)


def kernel(x, norm_g, w_in, hgrn_lb_logits, fox_fb, nsa_cmp_pe_k, nsa_cmp_w1_k, nsa_cmp_w2_k,
           nsa_cmp_pe_v, nsa_cmp_w1_v, nsa_cmp_w2_v, out_norm_g, w_out, final_norm_g):
    B, T, D = x.shape
    depth = w_in.shape[0]
    x2 = x.reshape(B * T, D)
    for l in range(depth):
        g_hg, g_fx, g_sb, g_ns, g_gate, g_sm = _project(x2, norm_g[l], _pack_in_weights(w_in[l]), 512)
        g_sm3 = g_sm.reshape(B, T, LANES)
        fb = jnp.concatenate([fox_fb[l], jnp.zeros((LANES - GROUP_HEADS,), F32)]).reshape(1, LANES)
        pe, w1, w2 = _pack_cmp_weights(nsa_cmp_pe_k[l], nsa_cmp_w1_k[l], nsa_cmp_w2_k[l],
                                       nsa_cmp_pe_v[l], nsa_cmp_w1_v[l], nsa_cmp_w2_v[l])
        y_hg = _hgrn2(g_hg.reshape(B, T, -1), hgrn_lb_logits, l, B, T)
        y_fx = _fox(g_fx.reshape(B, T, -1), g_sm3, fb, B, T)
        y_sb = _stick_breaking(g_sb.reshape(B, T, -1), B, T)
        y_ns = _nsa(g_ns.reshape(B, T, -1), g_sm3, pe, w1, w2, B, T)
        ys = [y.reshape(B * T, GROUP_W) for y in (y_hg, y_fx, y_sb, y_ns)]
        x2 = _out_project(ys, g_gate, x2, out_norm_g[l], w_out[l].astype(BF16), final_norm_g,
                          l == depth - 1, 512)
    return x2.reshape(B, T, D)
```

```python
import functools

import numpy as np
import jax
import jax.numpy as jnp
from jax import lax
from jax.experimental import pallas as pl
from jax.experimental.pallas import tpu as pltpu

F32 = jnp.float32
BF16 = jnp.bfloat16

D_MODEL = 1024
HEAD_DIM = 64
GROUP_W = 256
GROUP_HEADS = 4
ATTN_SCALE = HEAD_DIM ** -0.5
LOG2E = 1.4426950408889634
SCALE_LOG2 = ATTN_SCALE * LOG2E
ROPE_THETA = 500000.0
ROT_DIM = HEAD_DIM // 4
HG_CHUNK = 64
NSA_CMP_LEN = 32
NSA_CMP_STRIDE = 16
NSA_SEL_BLOCK = 64
NSA_TOP_N = 16
NSA_WINDOW = 512
NSA_FORCE = 1.0e4
NEG_BIG = -1.0e30
SEL_OFF = -(2.0 ** 100)
EPS = 1e-6

LANES = 128
VMEM_LIMIT = 56 * 1024 * 1024

IN_WIDTHS = (GROUP_W,) * 3 + (GROUP_W,) * 3 + (GROUP_HEADS,) + (GROUP_W,) * 3 + (GROUP_W,) \
    + (HEAD_DIM,) * 6 + (3 * GROUP_HEADS,) + (D_MODEL,)
IN_OFFS = np.concatenate([[0], np.cumsum(IN_WIDTHS)]).tolist()


def _nn(a, b):
    return jnp.dot(a, b, preferred_element_type=F32)


def _nt(a, b):
    return lax.dot_general(a, b, (((1,), (1,)), ((), ())), preferred_element_type=F32)


def _tn(a, b):
    return lax.dot_general(a, b, (((0,), (0,)), ((), ())), preferred_element_type=F32)


def _iota(shape, dim):
    return lax.broadcasted_iota(jnp.int32, shape, dim)


def _split2(x):
    hi = x.astype(BF16)
    lo = (x - hi.astype(F32)).astype(BF16)
    return hi, lo


def _split3(x):
    p1 = x.astype(BF16).astype(F32)
    r = x - p1
    p2 = r.astype(BF16).astype(F32)
    p3 = (r - p2).astype(BF16).astype(F32)
    return p1, p2, p3


def _sigmoid(x):
    return 1.0 / (1.0 + jnp.exp(-x))


def _log_sigmoid(x):
    return jnp.minimum(x, 0.0) - jnp.log(1.0 + jnp.exp(-jnp.abs(x)))


def _div(x, n):
    assert n & (n - 1) == 0
    return x >> (n.bit_length() - 1)


def _rows(i, n):
    return pl.ds(pl.multiple_of(i * n, n), n)


def _head_to_slot(x, h):
    return x if h % 2 == 0 else pltpu.roll(x, HEAD_DIM, 1)


def _proj_kernel(x_ref, g_ref, w_hg, w_fx, w_sb, w_ns, w_gate, w_sm,
                 o_hg, o_fx, o_sb, o_ns, o_gate, o_sm):
    x = x_ref[...]
    h = x * lax.rsqrt(jnp.mean(x * x, axis=-1, keepdims=True) + EPS) * g_ref[...]
    hb = h.astype(BF16)
    for w, o in ((w_hg, o_hg), (w_fx, o_fx), (w_sb, o_sb), (w_ns, o_ns), (w_gate, o_gate), (w_sm, o_sm)):
        o[...] = _nn(hb, w[...])


def _project(x2, norm_g, ws, tm):
    n = x2.shape[0]
    widths = [w.shape[1] for w in ws]
    return pl.pallas_call(
        _proj_kernel,
        grid=(n // tm,),
        in_specs=[pl.BlockSpec((tm, D_MODEL), lambda i: (i, 0)),
                  pl.BlockSpec((1, D_MODEL), lambda i: (0, 0))]
        + [pl.BlockSpec((D_MODEL, wd), lambda i: (0, 0)) for wd in widths],
        out_specs=[pl.BlockSpec((tm, wd), lambda i: (i, 0)) for wd in widths],
        out_shape=[jax.ShapeDtypeStruct((n, wd), F32) for wd in widths],
        compiler_params=pltpu.CompilerParams(dimension_semantics=("arbitrary",),
                                             vmem_limit_bytes=VMEM_LIMIT),
        name="proj",
    )(x2, norm_g.reshape(1, D_MODEL), *ws)


def _fox_kernel(g_ref, sm_ref, fb_ref, o_ref, cum_ref, qa_ref, ka_ref, va_ref, m_ref, acc_ref, *, T, TQ, TK):
    H = GROUP_HEADS
    RB = 256
    NB = T // LANES
    tri = jnp.where(_iota((LANES, LANES), 0) >= _iota((LANES, LANES), 1), 1.0, 0.0).astype(BF16)
    fb = fb_ref[...]

    local = []
    for i in range(NB):
        p1, p2, p3 = _split3(_log_sigmoid(sm_ref[i * LANES:(i + 1) * LANES, :] + fb))
        local.append(_nn(tri, p1.astype(BF16)) + _nn(tri, p2.astype(BF16)) + _nn(tri, p3.astype(BF16)))
    carry = jnp.zeros((1, LANES), F32)
    for i in range(NB):
        c = local[i] + carry
        cum_ref[i * LANES:(i + 1) * LANES, :] = c
        carry = c[LANES - 1:LANES, :]

    lane = _iota((RB, LANES), 1)
    upper = lane >= HEAD_DIM

    def build_body(i, _):
        r = _rows(i, RB)
        cum = cum_ref[r, :]
        for h in range(H):
            j = h // 2
            own = upper if h % 2 else ~upper
            a0 = 0 if h % 2 else HEAD_DIM
            q = g_ref[r, j * LANES:(j + 1) * LANES] * SCALE_LOG2
            k = g_ref[r, GROUP_W + j * LANES:GROUP_W + (j + 1) * LANES]
            v = g_ref[r, 2 * GROUP_W + j * LANES:2 * GROUP_W + (j + 1) * LANES]
            qa = jnp.where(own, q, 0.0)
            ka = jnp.where(own, k, 0.0)
            for p, c in enumerate(_split3(jnp.broadcast_to(cum[:, h:h + 1], (RB, LANES)) * LOG2E)):
                qa = jnp.where(lane == a0 + p, c, qa)
                qa = jnp.where(lane == a0 + 3 + p, 1.0, qa)
                ka = jnp.where(lane == a0 + p, 1.0, ka)
                ka = jnp.where(lane == a0 + 3 + p, -c, ka)
            qa_ref[h, r, :] = qa.astype(BF16)
            ka_ref[h, r, :] = ka.astype(BF16)
            va_ref[h, r, :] = jnp.where(own, v, 1.0).astype(BF16)
        return 0

    lax.fori_loop(0, T // RB, build_body, 0)

    NSUB = TQ // TK
    upper_q =_iota((TQ, LANES), 1) >= HEAD_DIM

    def q_body(i, _):
        rq = _rows(i, TQ)
        qas = [qa_ref[h, rq, :] for h in range(H)]
        for h in range(H):
            m_ref[h] = jnp.full((TQ, LANES), NEG_BIG, F32)
            acc_ref[h] = jnp.zeros((TQ, LANES), F32)

        def k_step(j, masked, r0=0):
            rk = _rows(j, TK)
            rs = slice(r0, TQ)
            for h in range(H):
                s = _nt(qas[h][rs], ka_ref[h, rk, :])
                if masked:
                    off = _iota((TQ - r0, TK), 1) - _iota((TQ - r0, TK), 0)
                    s = jnp.where(off <= i * TQ + r0 - j * TK, s, NEG_BIG)
                m = m_ref[h, rs, :]
                m_new = jnp.maximum(m, jnp.max(s, axis=-1, keepdims=True))
                p = jnp.exp2(jnp.concatenate([s[:, c0:c0 + LANES] - m_new for c0 in range(0, TK, LANES)], axis=1))
                acc_ref[h, rs, :] = jnp.exp2(m - m_new) * acc_ref[h, rs, :] + _nn(p.astype(BF16), va_ref[h, rk, :])
                m_ref[h, rs, :] = m_new

        def k_body(jj, _):
            for d in range(NSUB):
                k_step(jj * NSUB + d, False)
            return 0

        lax.fori_loop(0, i, k_body, 0)
        for d in range(NSUB):
            k_step(i * NSUB + d, True, r0=d * TK)
        for j in range(H // 2):
            ae, ao = acc_ref[2 * j], acc_ref[2 * j + 1]
            den = pltpu.roll(jnp.where(upper_q, ae, ao), HEAD_DIM, 1)
            o_ref[rq, j * LANES:(j + 1) * LANES] = jnp.where(upper_q, ao, ae) / den
        return 0

    lax.fori_loop(0, T // TQ, q_body, 0)


def _fox(g_fx, g_sm, fb, B, T):
    TQ, TK = 512, 256
    kern = functools.partial(_fox_kernel, T=T, TQ=TQ, TK=TK)
    return pl.pallas_call(
        kern,
        grid=(B,),
        in_specs=[pl.BlockSpec((None, T, 3 * GROUP_W), lambda b: (b, 0, 0)),
                  pl.BlockSpec((None, T, LANES), lambda b: (b, 0, 0)),
                  pl.BlockSpec((1, LANES), lambda b: (0, 0))],
        out_specs=pl.BlockSpec((None, T, GROUP_W), lambda b: (b, 0, 0)),
        out_shape=jax.ShapeDtypeStruct((B, T, GROUP_W), F32),
        scratch_shapes=[pltpu.VMEM((T, LANES), F32)]
        + [pltpu.VMEM((GROUP_HEADS, T, LANES), BF16) for _ in range(3)]
        + [pltpu.VMEM((GROUP_HEADS, TQ, LANES), F32) for _ in range(2)],
        compiler_params=pltpu.CompilerParams(dimension_semantics=("arbitrary",),
                                             vmem_limit_bytes=VMEM_LIMIT),
        name="fox",
    )(g_fx, g_sm, fb)


def _sb_kernel(g_ref, o_ref, qa_ref, ka_ref, va_ref, c_ref, acc_ref, *, T, TQ, TK):
    H = GROUP_HEADS
    RB = 256
    SUB = LANES
    NSUB = TQ // TK
    upper = _iota((RB, LANES), 1) >= HEAD_DIM

    def build_body(i, _):
        r = _rows(i, RB)
        for h in range(H):
            j = h // 2
            own = upper if h % 2 else ~upper
            q = g_ref[r, j * LANES:(j + 1) * LANES] * ATTN_SCALE
            k = g_ref[r, GROUP_W + j * LANES:GROUP_W + (j + 1) * LANES]
            v = g_ref[r, 2 * GROUP_W + j * LANES:2 * GROUP_W + (j + 1) * LANES]
            qa_ref[h, r, :] = jnp.where(own, q, 0.0).astype(BF16)
            ka_ref[h, r, :] = jnp.where(own, k, 0.0).astype(BF16)
            va_ref[h, r, :] = jnp.where(own, v, 0.0).astype(BF16)
        return 0

    lax.fori_loop(0, T // RB, build_body, 0)

    wr = _iota((2 * SUB, 2 * SUB), 0) & (SUB - 1)
    wc = _iota((2 * SUB, 2 * SUB), 1)
    suf_w = jnp.where((wc >= SUB) | (wr > wc), 1.0, 0.0).astype(BF16)

    def q_body(i, _):
        rq = _rows(i, TQ)
        qas = [qa_ref[h, rq, :] for h in range(H)]
        for h in range(H):
            c_ref[h] = jnp.zeros((TQ, LANES), F32)
        for j in range(H // 2):
            acc_ref[j] = jnp.zeros((TQ, LANES), F32)

        def k_step(j, masked, r0=0):
            rk = _rows(j, TK)
            rs = slice(r0, TQ)
            for h in range(H):
                z = _nt(qas[h][rs], ka_ref[h, rk, :])
                ls = jnp.minimum(z, 0.0) - jnp.log(1.0 + jnp.exp(-jnp.abs(z)))
                lom = ls - z
                if masked:
                    off = _iota((TQ - r0, TK), 1) - _iota((TQ - r0, TK), 0)
                    msk = off < i * TQ + r0 - j * TK
                    lom = jnp.where(msk, lom, 0.0)
                c = c_ref[h, rs, :]
                parts = []
                for sb in reversed(range(TK // SUB)):
                    hi, lo = _split2(lom[:, sb * SUB:(sb + 1) * SUB])
                    cs = _nn(jnp.concatenate([hi, lo], axis=1), suf_w)
                    parts.append(ls[:, sb * SUB:(sb + 1) * SUB] + cs[:, :SUB] + c)
                    c = c + cs[:, SUB:]
                c_ref[h, rs, :] = c
                a = jnp.exp(jnp.concatenate(parts[::-1], axis=1))
                if masked:
                    a = jnp.where(msk, a, 0.0)
                acc_ref[h // 2, rs, :] += _nn(a.astype(BF16), va_ref[h, rk, :])

        for d in reversed(range(NSUB)):
            k_step(i * NSUB + d, True, r0=d * TK)

        def k_body(jj, _):
            for d in range(NSUB):
                k_step((i - jj) * NSUB - 1 - d, False)
            return 0

        lax.fori_loop(0, i, k_body, 0)
        for j in range(H // 2):
            o_ref[rq, j * LANES:(j + 1) * LANES] = acc_ref[j]
        return 0

    lax.fori_loop(0, T // TQ, q_body, 0)


def _stick_breaking(g_sb, B, T):
    TQ, TK = 512, 256
    kern = functools.partial(_sb_kernel, T=T, TQ=TQ, TK=TK)
    return pl.pallas_call(
        kern,
        grid=(B,),
        in_specs=[pl.BlockSpec((None, T, 3 * GROUP_W), lambda b: (b, 0, 0))],
        out_specs=pl.BlockSpec((None, T, GROUP_W), lambda b: (b, 0, 0)),
        out_shape=jax.ShapeDtypeStruct((B, T, GROUP_W), F32),
        scratch_shapes=[pltpu.VMEM((GROUP_HEADS, T, LANES), BF16) for _ in range(3)]
        + [pltpu.VMEM((GROUP_HEADS, TQ, LANES), F32), pltpu.VMEM((GROUP_HEADS // 2, TQ, LANES), F32)],
        compiler_params=pltpu.CompilerParams(dimension_semantics=("arbitrary",),
                                             vmem_limit_bytes=VMEM_LIMIT),
        name="stickbreak",
    )(g_sb)


def _hgrn_decay_matrix(C):
    nl = C.bit_length() - 1
    t = _iota((C, C), 0)
    u = _iota((C, C), 1)
    blocks = [u <= t, u > t]
    q_blocks, k_blocks = [], []
    for l in range(nl):
        m = C >> (l + 1)
        bnd = (t & ~(2 * m - 1)) + (m - 1)
        q_blocks.append((u > bnd) & (u <= t))
        k_blocks.append((u > t) & (u <= bnd))
    d = jnp.concatenate([jnp.where(b, 1.0, 0.0) for b in blocks + q_blocks + k_blocks], axis=0).astype(BF16)
    return jnp.concatenate([d, d, d], axis=1)


def _hgrn_kernel(g_ref, lb_ref, o_ref, st_ref, ex_ref, *, T, C, NCH, layer):
    H = GROUP_HEADS
    nl = C.bit_length() - 1
    lg = lb_ref[...]
    e = jnp.exp(lg - jnp.max(lg, axis=0, keepdims=True))
    sm = e / jnp.sum(e, axis=0, keepdims=True)
    lb = jnp.sum(sm[0:layer + 1, :], axis=0, keepdims=True) - sm[0:1, :]

    dmat = _hgrn_decay_matrix(C)
    lane_head4 = _div(_iota((H * C, GROUP_W), 1), HEAD_DIM)
    row_head4 = _div(_iota((H * C, GROUP_W), 0), C)
    head_sel = lane_head4 == row_head4
    tq_c = _iota((C, GROUP_W), 0)
    t4 = _iota((C, H * C), 0)
    s4 = _iota((C, H * C), 1) & (C - 1)
    bd_mask =_div(_iota((GROUP_W, GROUP_W), 0), HEAD_DIM) == _div(_iota((GROUP_W, GROUP_W), 1), HEAD_DIM)
    st_ref[...] = jnp.zeros((GROUP_W, GROUP_W), F32)

    def decays(r, slot, n):
        g = jnp.log(lb + (1.0 - lb) * _sigmoid(g_ref[r, GROUP_W:2 * GROUP_W]))
        g1, g2, g3 = _split3(g)
        ex_ref[slot, n] = _nn(dmat, jnp.concatenate([g1.astype(BF16), g2.astype(BF16), g3.astype(BF16)], axis=0))

    def intra(r, slot, n):
        ex = ex_ref.at[slot, n]
        qraw = g_ref[r, 0:GROUP_W]
        f = g_ref[r, GROUP_W:2 * GROUP_W]
        v = g_ref[r, 2 * GROUP_W:3 * GROUP_W].astype(BF16)
        q = qraw * _sigmoid(qraw)
        k = (1.0 - lb) * (1.0 - _sigmoid(f))
        b = ex[0:C]

        a = jnp.zeros((C, H * C), F32)
        for l in range(nl + 1):
            if l < nl:
                m = C >> (l + 1)
                dq = ex[(2 + l) * C:(3 + l) * C]
                dk = ex[(2 + nl + l) * C:(3 + nl + l) * C]
                qs = q * jnp.where((tq_c & m) != 0, jnp.exp(dq), 0.0)
                ks = k * jnp.where((tq_c & m) == 0, jnp.exp(dk), 0.0)
                pair = (t4 & ~(2 * m - 1)) == (s4 & ~(2 * m - 1))
            else:
                qs, ks = q, k
                pair = t4 == s4
            ksb = ks.astype(BF16)
            kst = jnp.where(head_sel, jnp.concatenate([ksb] * H, axis=0), jnp.zeros((), BF16))
            a = a + jnp.where(pair, _nt(qs.astype(BF16), kst), 0.0)
        vbd = jnp.where(head_sel, jnp.concatenate([v] * H, axis=0), jnp.zeros((), BF16))
        o_intra = _nn(a.astype(BF16), vbd)
        qb =(q * jnp.exp(b)).astype(BF16)
        kd = (k * jnp.exp(ex[C:2 * C])).astype(BF16)
        upd = jnp.where(bd_mask, _tn(v, kd), 0.0)
        return o_intra, qb, upd, jnp.exp(b[C - 1:C, :])

    NIT = T // (C * NCH)
    for n in range(NCH):
        decays(_rows(n, C), 0, n)

    def body(ci, _):
        slot = ci & 1
        nxt = jnp.minimum(ci + 1, NIT - 1)
        for n in range(NCH):
            decays(_rows(nxt * NCH + n, C), 1 - slot, n)
        rs = [_rows(ci * NCH + n, C) for n in range(NCH)]
        parts = [intra(r, slot, n) for n, r in enumerate(rs)]
        st = st_ref[...]
        for r, (o_intra, qb, upd, decay) in zip(rs, parts):
            o_ref[r, :] = o_intra + _nt(qb, st.astype(BF16))
            st = st * decay + upd
        st_ref[...] = st
        return 0

    lax.fori_loop(0, NIT, body, 0)


def _hgrn2(g_hg, lb_logits, layer, B, T):
    depth = lb_logits.shape[0]
    NCH = 2
    kern = functools.partial(_hgrn_kernel, T=T, C=HG_CHUNK, NCH=NCH, layer=layer)
    return pl.pallas_call(
        kern,
        grid=(B,),
        in_specs=[pl.BlockSpec((None, T, 3 * GROUP_W), lambda b: (b, 0, 0)),
                  pl.BlockSpec((depth, GROUP_W), lambda b: (0, 0))],
        out_specs=pl.BlockSpec((None, T, GROUP_W), lambda b: (b, 0, 0)),
        out_shape=jax.ShapeDtypeStruct((B, T, GROUP_W), F32),
        scratch_shapes=[pltpu.VMEM((GROUP_W, GROUP_W), F32),
                        pltpu.VMEM((2, NCH, 2 * HG_CHUNK.bit_length() * HG_CHUNK, GROUP_W), F32)],
        compiler_params=pltpu.CompilerParams(dimension_semantics=("arbitrary",),
                                             vmem_limit_bytes=VMEM_LIMIT),
        name="hgrn2",
    )(g_hg, lb_logits)


def _rope(x, cos, sin, lane):
    swapped = jnp.where((lane & (HEAD_DIM - 1)) < ROT_DIM // 2,
                        pltpu.roll(x, LANES - ROT_DIM // 2, 1), pltpu.roll(x, ROT_DIM // 2, 1))
    return x * cos + swapped * sin


def _nsa_kernel(g_ref, sm_ref, cosq_ref, sinq_ref, cosk_ref, sink_ref, cosc_ref, sinc_ref,
                pe_ref, w1_ref, w2_ref, o_ref,
                cv_ref, ablk_ref, qa_ref, kcmp_ref, vcmp_ref, ksel_ref, vsel_ref, kwin_ref, vwin_ref,
                m_ref, acc_ref, score_ref, *, T, TQ, TK):
    H = GROUP_HEADS
    RB = 256
    NC = T // NSA_CMP_STRIDE
    NSEL = T // NSA_SEL_BLOCK
    TOPN = min(NSA_TOP_N, NSEL)
    WT = NSA_WINDOW // TK
    lane = _iota((RB, LANES), 1)
    grow = _iota((RB, LANES), 0)

    def build_body(i, _):
        r = _rows(i, RB)
        cosq, sinq = cosq_ref[r, :], sinq_ref[r, :]
        cosk, sink = cosk_ref[r, :], sink_ref[r, :]
        for j in range(2):
            xr = _rope(g_ref[r, j * LANES:(j + 1) * LANES], cosq, sinq, lane) * SCALE_LOG2
            for h in (2 * j, 2 * j + 1):
                qa_ref[h, r, :] = jnp.where(lane < 64, _head_to_slot(xr, h), 0.0).astype(BF16)
        cv_ref[r, :] = g_ref[r, GROUP_W:GROUP_W + LANES]
        ksvs = g_ref[r, GROUP_W + LANES:GROUP_W + 2 * LANES]
        kwvw = g_ref[r, GROUP_W + 2 * LANES:GROUP_W + 3 * LANES]
        blk = _div(i * RB + grow, NSA_SEL_BLOCK)
        onehot = jnp.where((lane >= 64) & (lane - 64 == blk), 1.0, 0.0)
        ksel_ref[r, :] = jnp.where(lane < 64, _rope(ksvs, cosk, sink, lane), onehot).astype(BF16)
        vsel_ref[r, :] = jnp.where(lane < 64, pltpu.roll(ksvs, 64, 1), 1.0).astype(BF16)
        kwin_ref[r, :] = jnp.where(lane < 64, _rope(kwvw, cosk, sink, lane), 0.0).astype(BF16)
        vwin_ref[r, :] = jnp.where(lane < 64, pltpu.roll(kwvw, 64, 1), 1.0).astype(BF16)
        return 0

    lax.fori_loop(0, T // RB, build_body, 0)
    cv_ref[T:T + 2 * NSA_CMP_STRIDE, :] = jnp.zeros((2 * NSA_CMP_STRIDE, LANES), F32)

    for l in range(NSA_CMP_LEN):
        blk_l = cv_ref[pl.ds(l, NC, stride=NSA_CMP_STRIDE), :] + pe_ref[l:l + 1, :]
        ablk_ref[:, l * LANES:(l + 1) * LANES] = blk_l.astype(BF16)
    hid = _nn(ablk_ref[...], w1_ref[...])
    hid = hid * _sigmoid(hid)
    kv = _nn(hid.astype(BF16), w2_ref[...])
    lane_c = _iota((NC, LANES), 1)
    kcmp_ref[...] = jnp.where(lane_c < 64, _rope(kv, cosc_ref[...], sinc_ref[...], lane_c), 0.0).astype(BF16)
    vcmp_ref[...] = jnp.where(lane_c < 64, pltpu.roll(kv, 64, 1), 0.0).astype(BF16)

    on = _iota((NSEL, 2 * NC), 1) & (NC - 1)
    oj = _iota((NSEL, 2 * NC), 0)
    ovt = jnp.clip(jnp.minimum(on * NSA_CMP_STRIDE + NSA_CMP_LEN, oj * NSA_SEL_BLOCK + NSA_SEL_BLOCK)
                   - jnp.maximum(on * NSA_CMP_STRIDE, oj * NSA_SEL_BLOCK), 0, None).astype(F32) / NSA_CMP_LEN
    ovt = jnp.where(on < NC - 1, ovt, 0.0).astype(BF16)
    jrow = _iota((NSEL, TQ), 0)
    tcol = _iota((NSEL, TQ), 1)

    gr = _iota((2 * LANES, 3 * GROUP_W), 0) & (LANES - 1)
    gc = _iota((2 * LANES, 3 * GROUP_W), 1)
    gate_w = jnp.where(gr == GROUP_HEADS + _div(gc, GROUP_W) * GROUP_HEADS + _div(gc & (GROUP_W - 1), HEAD_DIM),
                       1.0, 0.0).astype(BF16)

    row4 = _iota((H * TQ, TK), 0) & (TQ - 1)
    col4 = _iota((H * TQ, TK), 1)
    lane4 = _iota((H * TQ, LANES), 1)
    lane_q = _iota((TQ, LANES), 1)
    row_q = _iota((TQ, LANES), 0)
    cmp_end = _iota((H * TQ, NC), 1) * NSA_CMP_STRIDE + (NSA_CMP_LEN - 1)
    rowc = _iota((H * TQ, NC), 0) & (TQ - 1)

    def q_body(i, _):
        rq = _rows(i, TQ)
        t0 = i * TQ
        qst = jnp.concatenate([qa_ref[h, rq, :] for h in range(H)], axis=0)

        valid = cmp_end <= (t0 + rowc)
        s = jnp.where(valid, _nt(qst, kcmp_ref[...]), NEG_BIG)
        p = jnp.where(valid, jnp.exp2(s - jnp.max(s, axis=-1, keepdims=True)), 0.0)
        den = jnp.sum(p, axis=-1, keepdims=True)
        p = p / jnp.where(den > 0, den, 1.0)
        o_cmp = _nn(p.astype(BF16), vcmp_ref[...])
        psum = p[0:TQ]
        for h in range(1, H):
            psum = psum + p[h * TQ:(h + 1) * TQ]
        hi, lo = _split2(psum)
        imp = _nt(ovt, jnp.concatenate([hi, lo], axis=1))

        qblk = _div(t0 + tcol, NSA_SEL_BLOCK)
        forced = (jrow == 0) | (jrow == qblk) | (jrow == qblk - 1)
        score = jnp.where(forced, NSA_FORCE, imp)
        score_ref[...] = jnp.where(jrow <= qblk, score, -NSA_FORCE)
        score = score_ref[...]
        rank = jnp.zeros((NSEL, TQ), F32)
        for ii in range(NSEL):
            ci = score_ref[ii:ii + 1, :]
            rank = rank + jnp.where((ci > score) | ((ci == score) & (ii < jrow)), 1.0, 0.0)
        sel = (rank < TOPN) & (jrow <= qblk)
        bias = jnp.where(sel, 0.0, SEL_OFF)
        aug = jnp.concatenate([jnp.zeros((HEAD_DIM, TQ), F32), bias,
                               jnp.zeros((LANES - HEAD_DIM - NSEL, TQ), F32)], axis=0).T.astype(BF16)
        qsel = jnp.where(lane4 < 64, qst, jnp.concatenate([aug] * H, axis=0))

        def att_step(j, _, slot, q_in, k_ref, v_ref, mode):
            rk = _rows(j, TK)
            s = _nt(q_in, k_ref[rk, :])
            if mode is not None:
                kpos = j * TK + col4
                qpos = t0 + row4
                msk = (kpos <= qpos) if mode == "causal" else (kpos > qpos - NSA_WINDOW)
                s = jnp.where(msk, s, NEG_BIG)
            m = m_ref[slot]
            m_new = jnp.maximum(m, jnp.max(s, axis=-1, keepdims=True))
            p = jnp.exp2(jnp.concatenate([s[:, c0:c0 + LANES] - m_new for c0 in range(0, TK, LANES)], axis=1))
            if mode is not None:
                p = jnp.where(msk, p, 0.0)
            acc_ref[slot] = jnp.exp2(m - m_new) * acc_ref[slot] + _nn(p.astype(BF16), v_ref[rk, :])
            m_ref[slot] = m_new
            return 0

        for slot in range(2):
            m_ref[slot] = jnp.full((H * TQ, LANES), NEG_BIG, F32)
            acc_ref[slot] = jnp.zeros((H * TQ, LANES), F32)
        sel_step = functools.partial(att_step, slot=0, q_in=qsel, k_ref=ksel_ref, v_ref=vsel_ref)
        def sel_pair(jj, _):
            sel_step(2 * jj, 0, mode=None)
            sel_step(2 * jj + 1, 0, mode=None)
            return 0

        lax.fori_loop(0, i >> 1, sel_pair, 0)

        @pl.when((i & 1) == 1)
        def _():
            sel_step(i - 1, 0, mode=None)

        sel_step(i, 0, mode="causal")
        win_step = functools.partial(att_step, slot=1, q_in=qst, k_ref=kwin_ref, v_ref=vwin_ref)
        lax.fori_loop(jnp.maximum(i - WT, 0), i, functools.partial(win_step, mode="window"), 0)
        win_step(i, 0, mode="causal")

        hi, lo = _split2(_sigmoid(sm_ref[rq, :]))
        gmap = _nn(jnp.concatenate([hi, lo], axis=1), gate_w)
        low = lane_q < 64
        for j in range(H // 2):
            re = slice(2 * j * TQ, (2 * j + 1) * TQ)
            ro = slice((2 * j + 1) * TQ, (2 * j + 2) * TQ)
            out = gmap[:, j * LANES:(j + 1) * LANES] * jnp.where(low, o_cmp[re], pltpu.roll(o_cmp[ro], 64, 1))
            for slot in range(2):
                ae, ao = acc_ref[slot, re, :], acc_ref[slot, ro, :]
                num = jnp.where(low, ae, pltpu.roll(ao, 64, 1))
                den = jnp.where(low, pltpu.roll(ae, 64, 1), ao)
                c0 = (slot + 1) * GROUP_W + j * LANES
                out = out + gmap[:, c0:c0 + LANES] * (num / den)
            o_ref[rq, j * LANES:(j + 1) * LANES] = out
        return 0

    lax.fori_loop(0, T // TQ, q_body, 0)


def _rope_tables(pos):
    half = ROT_DIM // 2
    inv_freq = ROPE_THETA ** (-(jnp.arange(half, dtype=F32) * 2.0 / ROT_DIM))
    ang = pos.astype(F32)[:, None] * inv_freq[None, :]
    cos, sin = jnp.cos(ang), jnp.sin(ang)
    n = pos.shape[0]
    c64 = jnp.concatenate([cos, cos, jnp.ones((n, HEAD_DIM - ROT_DIM), F32)], axis=1)
    s64 = jnp.concatenate([-sin, sin, jnp.zeros((n, HEAD_DIM - ROT_DIM), F32)], axis=1)
    return c64, s64


def _nsa(g_ns, g_sm, pe, w1, w2, B, T):
    TQ = TK = 256
    NC = T // NSA_CMP_STRIDE
    c64, s64 = _rope_tables(jnp.arange(T))
    cosq, sinq = jnp.tile(c64, (1, 2)), jnp.tile(s64, (1, 2))
    cosk = jnp.concatenate([c64, jnp.ones((T, HEAD_DIM), F32)], axis=1)
    sink = jnp.concatenate([s64, jnp.zeros((T, HEAD_DIM), F32)], axis=1)
    cc, sc = _rope_tables(jnp.arange(NC) * NSA_CMP_STRIDE + NSA_CMP_LEN - 1)
    cosc = jnp.concatenate([cc, jnp.ones((NC, HEAD_DIM), F32)], axis=1)
    sinc = jnp.concatenate([sc, jnp.zeros((NC, HEAD_DIM), F32)], axis=1)
    kern = functools.partial(_nsa_kernel, T=T, TQ=TQ, TK=TK)
    full = lambda shape: pl.BlockSpec(shape, lambda b: (0,) * len(shape))
    ns_w = GROUP_W + 3 * LANES
    return pl.pallas_call(
        kern,
        grid=(B,),
        in_specs=[pl.BlockSpec((None, T, ns_w), lambda b: (b, 0, 0)),
                  pl.BlockSpec((None, T, LANES), lambda b: (b, 0, 0)),
                  full((T, LANES)), full((T, LANES)), full((T, LANES)), full((T, LANES)),
                  full((NC, LANES)), full((NC, LANES)),
                  full((NSA_CMP_LEN, LANES)), full((NSA_CMP_LEN * LANES, LANES)), full((LANES, LANES))],
        out_specs=pl.BlockSpec((None, T, GROUP_W), lambda b: (b, 0, 0)),
        out_shape=jax.ShapeDtypeStruct((B, T, GROUP_W), F32),
        scratch_shapes=[pltpu.VMEM((T + 2 * NSA_CMP_STRIDE, LANES), F32),
                        pltpu.VMEM((NC, NSA_CMP_LEN * LANES), BF16),
                        pltpu.VMEM((GROUP_HEADS, T, LANES), BF16),
                        pltpu.VMEM((NC, LANES), BF16), pltpu.VMEM((NC, LANES), BF16)]
        + [pltpu.VMEM((T, LANES), BF16) for _ in range(4)]
        + [pltpu.VMEM((2, GROUP_HEADS * TQ, LANES), F32) for _ in range(2)]
        + [pltpu.VMEM((T // NSA_SEL_BLOCK, TQ), F32)],
        compiler_params=pltpu.CompilerParams(dimension_semantics=("arbitrary",),
                                             vmem_limit_bytes=VMEM_LIMIT),
        name="nsa",
    )(g_ns, g_sm, cosq, sinq, cosk, sink, cosc, sinc, pe, w1, w2)


def _out_kernel(y_hg, y_fx, y_sb, y_ns, gate_ref, x_ref, gn_ref, w_ref, fg_ref, o_ref, *, final):
    bd = jnp.where(_div(_iota((LANES, LANES), 0), HEAD_DIM) == _div(_iota((LANES, LANES), 1), HEAD_DIM),
                   1.0, 0.0).astype(BF16)
    parts = []
    for ref in (y_hg, y_fx, y_sb, y_ns):
        y = ref[...]
        for j in range(GROUP_W // LANES):
            yy = y[:, j * LANES:(j + 1) * LANES]
            hi, lo = _split2(yy * yy)
            ms = (_nn(hi, bd) + _nn(lo, bd)) * (1.0 / HEAD_DIM)
            parts.append(yy * lax.rsqrt(ms + EPS))
    gt = gate_ref[...]
    z = jnp.concatenate(parts, axis=1) * gn_ref[...] * (gt * _sigmoid(gt))
    out = x_ref[...] + _nn(z.astype(BF16), w_ref[...])
    if final:
        out = out * lax.rsqrt(jnp.mean(out * out, axis=-1, keepdims=True) + EPS) * fg_ref[...]
    o_ref[...] = out


def _out_project(ys, gate, x2, gn, w_out, fg, final, tm):
    n = x2.shape[0]
    kern = functools.partial(_out_kernel, final=final)
    row = lambda wd: pl.BlockSpec((tm, wd), lambda i: (i, 0))
    return pl.pallas_call(
        kern,
        grid=(n // tm,),
        in_specs=[row(GROUP_W)] * 4 + [row(D_MODEL), row(D_MODEL),
                                       pl.BlockSpec((1, D_MODEL), lambda i: (0, 0)),
                                       pl.BlockSpec((D_MODEL, D_MODEL), lambda i: (0, 0)),
                                       pl.BlockSpec((1, D_MODEL), lambda i: (0, 0))],
        out_specs=row(D_MODEL),
        out_shape=jax.ShapeDtypeStruct((n, D_MODEL), F32),
        compiler_params=pltpu.CompilerParams(dimension_semantics=("arbitrary",),
                                             vmem_limit_bytes=VMEM_LIMIT),
        name="outproj",
    )(*ys, gate, x2, gn.reshape(1, D_MODEL), w_out, fg.reshape(1, D_MODEL))


def _pack_in_weights(w):
    o = IN_OFFS
    sm = jnp.concatenate([w[:, o[6]:o[7]], w[:, o[17]:o[18]],
                          jnp.zeros((D_MODEL, LANES - GROUP_HEADS - 3 * GROUP_HEADS), w.dtype)], axis=1)
    groups = [w[:, o[0]:o[3]], w[:, o[3]:o[6]], w[:, o[7]:o[10]], w[:, o[10]:o[17]], w[:, o[18]:o[19]], sm]
    return [g.astype(BF16) for g in groups]


def _pack_cmp_weights(pe_k, w1_k, w2_k, pe_v, w1_v, w2_v):
    L, d = NSA_CMP_LEN, HEAD_DIM
    z = jnp.zeros((L, d, d), F32)
    w1 = jnp.concatenate([jnp.concatenate([w1_k.reshape(L, d, d), z], axis=2),
                          jnp.concatenate([z, w1_v.reshape(L, d, d)], axis=2)], axis=1).reshape(L * 2 * d, 2 * d)
    z2 = jnp.zeros((d, d), F32)
    w2 = jnp.concatenate([jnp.concatenate([w2_k, z2], axis=1), jnp.concatenate([z2, w2_v], axis=1)], axis=0)
    return jnp.concatenate([pe_k, pe_v], axis=1), w1.astype(BF16), w2.astype(BF16)


def kernel(x, norm_g, w_in, hgrn_lb_logits, fox_fb, nsa_cmp_pe_k, nsa_cmp_w1_k, nsa_cmp_w2_k,
           nsa_cmp_pe_v, nsa_cmp_w1_v, nsa_cmp_w2_v, out_norm_g, w_out, final_norm_g):
    B, T, D = x.shape
    depth = w_in.shape[0]
    x2 = x.reshape(B * T, D)
    for l in range(depth):
        g_hg, g_fx, g_sb, g_ns, g_gate, g_sm = _project(x2, norm_g[l], _pack_in_weights(w_in[l]), 512)
        g_sm3 = g_sm.reshape(B, T, LANES)
        fb = jnp.concatenate([fox_fb[l], jnp.zeros((LANES - GROUP_HEADS,), F32)]).reshape(1, LANES)
        pe, w1, w2 = _pack_cmp_weights(nsa_cmp_pe_k[l], nsa_cmp_w1_k[l], nsa_cmp_w2_k[l],
                                       nsa_cmp_pe_v[l], nsa_cmp_w1_v[l], nsa_cmp_w2_v[l])
        y_hg = _hgrn2(g_hg.reshape(B, T, -1), hgrn_lb_logits, l, B, T)
        y_fx = _fox(g_fx.reshape(B, T, -1), g_sm3, fb, B, T)
        y_sb = _stick_breaking(g_sb.reshape(B, T, -1), B, T)
        y_ns = _nsa(g_ns.reshape(B, T, -1), g_sm3, pe, w1, w2, B, T)
        ys = [y.reshape(B * T, GROUP_W) for y in (y_hg, y_fx, y_sb, y_ns)]
        x2 = _out_project(ys, g_gate, x2, out_norm_g[l], w_out[l].astype(BF16), final_norm_g,
                          l == depth - 1, 512)
    return x2.reshape(B, T, D)
```

```python
import functools

import numpy as np
import jax
import jax.numpy as jnp
from jax import lax
from jax.experimental import pallas as pl
from jax.experimental.pallas import tpu as pltpu

F32 = jnp.float32
BF16 = jnp.bfloat16

D_MODEL = 1024
HEAD_DIM = 64
GROUP_W = 256
GROUP_HEADS = 4
ATTN_SCALE = HEAD_DIM ** -0.5
LOG2E = 1.4426950408889634
SCALE_LOG2 = ATTN_SCALE * LOG2E
ROPE_THETA = 500000.0
ROT_DIM = HEAD_DIM // 4
HG_CHUNK = 64
NSA_CMP_LEN = 32
NSA_CMP_STRIDE = 16
NSA_SEL_BLOCK = 64
NSA_TOP_N = 16
NSA_WINDOW = 512
NSA_FORCE = 1.0e4
NEG_BIG = -1.0e30
SEL_OFF = -(2.0 ** 100)
EPS = 1e-6

LANES = 128
VMEM_LIMIT = 56 * 1024 * 1024

IN_WIDTHS = (GROUP_W,) * 3 + (GROUP_W,) * 3 + (GROUP_HEADS,) + (GROUP_W,) * 3 + (GROUP_W,) \
    + (HEAD_DIM,) * 6 + (3 * GROUP_HEADS,) + (D_MODEL,)
IN_OFFS = np.concatenate([[0], np.cumsum(IN_WIDTHS)]).tolist()
GATE_GROUP = 4


def _nn(a, b):
    return jnp.dot(a, b, preferred_element_type=F32)


def _nt(a, b):
    return lax.dot_general(a, b, (((1,), (1,)), ((), ())), preferred_element_type=F32)


def _tn(a, b):
    return lax.dot_general(a, b, (((0,), (0,)), ((), ())), preferred_element_type=F32)


def _iota(shape, dim):
    return lax.broadcasted_iota(jnp.int32, shape, dim)


def _split2(x):
    hi = x.astype(BF16)
    lo = (x - hi.astype(F32)).astype(BF16)
    return hi, lo


def _split3(x):
    p1 = x.astype(BF16).astype(F32)
    r = x - p1
    p2 = r.astype(BF16).astype(F32)
    p3 = (r - p2).astype(BF16).astype(F32)
    return p1, p2, p3


def _sigmoid(x):
    return 1.0 / (1.0 + jnp.exp(-x))


def _log_sigmoid(x):
    return jnp.minimum(x, 0.0) - jnp.log(1.0 + jnp.exp(-jnp.abs(x)))


def _div(x, n):
    assert n & (n - 1) == 0
    return x >> (n.bit_length() - 1)


def _rows(i, n):
    return pl.ds(pl.multiple_of(i * n, n), n)


def _head_to_slot(x, h):
    return x if h % 2 == 0 else pltpu.roll(x, HEAD_DIM, 1)


def _proj_kernel(x_ref, g_ref, w_hg, w_fx, w_sb, w_ns, w_gate, w_sm,
                 o_hg, o_fx, o_sb, o_ns, o_gate, o_sm):
    x = x_ref[...]
    h = x * lax.rsqrt(jnp.mean(x * x, axis=-1, keepdims=True) + EPS) * g_ref[...]
    hb = h.astype(BF16)
    for w, o in ((w_hg, o_hg), (w_fx, o_fx), (w_sb, o_sb), (w_ns, o_ns), (w_gate, o_gate), (w_sm, o_sm)):
        o[...] = _nn(hb, w[...]).astype(o.dtype)


def _project(x2, norm_g, ws, tm):
    n = x2.shape[0]
    widths = [w.shape[1] for w in ws]
    return pl.pallas_call(
        _proj_kernel,
        grid=(n // tm,),
        in_specs=[pl.BlockSpec((tm, D_MODEL), lambda i: (i, 0)),
                  pl.BlockSpec((1, D_MODEL), lambda i: (0, 0))]
        + [pl.BlockSpec((D_MODEL, wd), lambda i: (0, 0)) for wd in widths],
        out_specs=[pl.BlockSpec((tm, wd), lambda i: (i, 0)) for wd in widths],
        out_shape=[jax.ShapeDtypeStruct((n, wd), BF16 if gi == GATE_GROUP else F32) for gi, wd in enumerate(widths)],
        compiler_params=pltpu.CompilerParams(dimension_semantics=("arbitrary",),
                                             vmem_limit_bytes=VMEM_LIMIT),
        name="proj",
    )(x2, norm_g.reshape(1, D_MODEL), *ws)


def _fox_program(g_ref, sm_ref, fb_ref, o_ref, cum_ref, qa_ref, ka_ref, va_ref, m_ref, acc_ref, *, T, TQ, TK):
    H = GROUP_HEADS
    RB = 256
    NB = T // LANES
    tri = jnp.where(_iota((LANES, LANES), 0) >= _iota((LANES, LANES), 1), 1.0, 0.0).astype(BF16)
    fb = fb_ref[...]

    local = []
    for i in range(NB):
        p1, p2, p3 = _split3(_log_sigmoid(sm_ref[i * LANES:(i + 1) * LANES, :] + fb))
        local.append(_nn(tri, p1.astype(BF16)) + _nn(tri, p2.astype(BF16)) + _nn(tri, p3.astype(BF16)))
    carry = jnp.zeros((1, LANES), F32)
    for i in range(NB):
        c = local[i] + carry
        cum_ref[i * LANES:(i + 1) * LANES, :] = c
        carry = c[LANES - 1:LANES, :]

    lane = _iota((RB, LANES), 1)
    upper = lane >= HEAD_DIM

    def build_body(i, _):
        r = _rows(i, RB)
        cum = cum_ref[r, :]
        for h in range(H):
            j = h // 2
            own = upper if h % 2 else ~upper
            a0 = 0 if h % 2 else HEAD_DIM
            q = g_ref[r, j * LANES:(j + 1) * LANES] * SCALE_LOG2
            k = g_ref[r, GROUP_W + j * LANES:GROUP_W + (j + 1) * LANES]
            v = g_ref[r, 2 * GROUP_W + j * LANES:2 * GROUP_W + (j + 1) * LANES]
            qa = jnp.where(own, q, 0.0)
            ka = jnp.where(own, k, 0.0)
            for p, c in enumerate(_split3(jnp.broadcast_to(cum[:, h:h + 1], (RB, LANES)) * LOG2E)):
                qa = jnp.where(lane == a0 + p, c, qa)
                qa = jnp.where(lane == a0 + 3 + p, 1.0, qa)
                ka = jnp.where(lane == a0 + p, 1.0, ka)
                ka = jnp.where(lane == a0 + 3 + p, -c, ka)
            qa_ref[h, r, :] = qa.astype(BF16)
            ka_ref[h, r, :] = ka.astype(BF16)
            va_ref[h, r, :] = jnp.where(own, v, 1.0).astype(BF16)
        return 0

    lax.fori_loop(0, T // RB, build_body, 0)

    NSUB = TQ // TK
    upper_q =_iota((TQ, LANES), 1) >= HEAD_DIM

    def start(i):
        for h in range(H):
            m_ref[h] = jnp.full((TQ, LANES), NEG_BIG, F32)
            acc_ref[h] = jnp.zeros((TQ, LANES), F32)
        return [qa_ref[h, _rows(i, TQ), :] for h in range(H)]

    def k_step(qas, i, j, masked, r0=0):
        rk = _rows(j, TK)
        rs = slice(r0, TQ)
        for h in range(H):
            s = _nt(qas[h][rs], ka_ref[h, rk, :])
            if masked:
                off = _iota((TQ - r0, TK), 1) - _iota((TQ - r0, TK), 0)
                s = jnp.where(off <= i * TQ + r0 - j * TK, s, NEG_BIG)
            m = m_ref[h, rs, :]
            m_new = jnp.maximum(m, jnp.max(s, axis=-1, keepdims=True))
            p = jnp.exp2(jnp.concatenate([s[:, c0:c0 + LANES] - m_new for c0 in range(0, TK, LANES)], axis=1))
            acc_ref[h, rs, :] = jnp.exp2(m - m_new) * acc_ref[h, rs, :] + _nn(p.astype(BF16), va_ref[h, rk, :])
            m_ref[h, rs, :] = m_new

    def diag(qas, i):
        for d in range(NSUB):
            k_step(qas, i, i * NSUB + d, True, r0=d * TK)

    def pair(qas, i, jj):
        for d in range(NSUB):
            k_step(qas, i, jj * NSUB + d, False)

    def finish(i):
        for j in range(H // 2):
            ae, ao = acc_ref[2 * j], acc_ref[2 * j + 1]
            den = pltpu.roll(jnp.where(upper_q, ae, ao), HEAD_DIM, 1)
            o_ref[_rows(i, TQ), j * LANES:(j + 1) * LANES] = (jnp.where(upper_q, ao, ae) / den).astype(BF16)

    return start, diag, pair, finish


def _sb_program(g_ref, o_ref, qa_ref, ka_ref, va_ref, c_ref, acc_ref, *, T, TQ, TK):
    H = GROUP_HEADS
    RB = 256
    SUB = LANES
    NSUB = TQ // TK
    upper = _iota((RB, LANES), 1) >= HEAD_DIM

    def build_body(i, _):
        r = _rows(i, RB)
        for h in range(H):
            j = h // 2
            own = upper if h % 2 else ~upper
            q = g_ref[r, j * LANES:(j + 1) * LANES] * ATTN_SCALE
            k = g_ref[r, GROUP_W + j * LANES:GROUP_W + (j + 1) * LANES]
            v = g_ref[r, 2 * GROUP_W + j * LANES:2 * GROUP_W + (j + 1) * LANES]
            qa_ref[h, r, :] = jnp.where(own, q, 0.0).astype(BF16)
            ka_ref[h, r, :] = jnp.where(own, k, 0.0).astype(BF16)
            va_ref[h, r, :] = jnp.where(own, v, 0.0).astype(BF16)
        return 0

    lax.fori_loop(0, T // RB, build_body, 0)

    wr = _iota((2 * SUB, 2 * SUB), 0) & (SUB - 1)
    wc = _iota((2 * SUB, 2 * SUB), 1)
    suf_w = jnp.where((wc >= SUB) | (wr > wc), 1.0, 0.0).astype(BF16)

    def start(i):
        for h in range(H):
            c_ref[h] = jnp.zeros((TQ, LANES), F32)
        for j in range(H // 2):
            acc_ref[j] = jnp.zeros((TQ, LANES), F32)
        return [qa_ref[h, _rows(i, TQ), :] for h in range(H)]

    def k_step(qas, i, j, masked, r0=0):
        rk = _rows(j, TK)
        rs = slice(r0, TQ)
        for h in range(H):
            z = _nt(qas[h][rs], ka_ref[h, rk, :])
            ls = jnp.minimum(z, 0.0) - jnp.log(1.0 + jnp.exp2(jnp.abs(z) * -LOG2E))
            lom = ls - z
            if masked:
                off = _iota((TQ - r0, TK), 1) - _iota((TQ - r0, TK), 0)
                msk = off < i * TQ + r0 - j * TK
                lom = jnp.where(msk, lom, 0.0)
            c = c_ref[h, rs, :]
            parts = []
            for sb in reversed(range(TK // SUB)):
                hi, lo = _split2(lom[:, sb * SUB:(sb + 1) * SUB])
                cs = _nn(jnp.concatenate([hi, lo], axis=1), suf_w)
                parts.append(ls[:, sb * SUB:(sb + 1) * SUB] + cs[:, :SUB] + c)
                c = c + cs[:, SUB:]
            c_ref[h, rs, :] = c
            a = jnp.exp(jnp.concatenate(parts[::-1], axis=1))
            if masked:
                a = jnp.where(msk, a, 0.0)
            acc_ref[h // 2, rs, :] += _nn(a.astype(BF16), va_ref[h, rk, :])

    def diag(qas, i):
        for d in reversed(range(NSUB)):
            k_step(qas, i, i * NSUB + d, True, r0=d * TK)

    def pair(qas, i, jj):
        for d in range(NSUB):
            k_step(qas, i, (i - jj) * NSUB - 1 - d, False)

    def finish(i):
        for j in range(H // 2):
            o_ref[_rows(i, TQ), j * LANES:(j + 1) * LANES] = acc_ref[j].astype(BF16)

    return start, diag, pair, finish


def _fox_sb_kernel(fx_ref, sm_ref, fb_ref, sb_ref, ofx_ref, osb_ref,
                   cum_ref, fqa, fka, fva, fm, facc, sqa, ska, sva, sc, sacc, *, T, TQ, TK):
    fox = _fox_program(fx_ref, sm_ref, fb_ref, ofx_ref, cum_ref, fqa, fka, fva, fm, facc, T=T, TQ=TQ, TK=TK)
    stb = _sb_program(sb_ref, osb_ref, sqa, ska, sva, sc, sacc, T=T, TQ=TQ, TK=TK)
    progs = (fox, stb)

    def q_body(i, _):
        ctx = [p[0](i) for p in progs]
        for p, qas in zip(progs, ctx):
            p[1](qas, i)

        def k_body(jj, _):
            for p, qas in zip(progs, ctx):
                p[2](qas, i, jj)
            return 0

        lax.fori_loop(0, i, k_body, 0)
        for p in progs:
            p[3](i)
        return 0

    lax.fori_loop(0, T // TQ, q_body, 0)


def _fox_stick_breaking(g_fx, g_sm, fb, g_sb, B, T):
    TQ, TK = 512, 256
    kern = functools.partial(_fox_sb_kernel, T=T, TQ=TQ, TK=TK)
    slot = lambda: pltpu.VMEM((GROUP_HEADS, T, LANES), BF16)
    state = lambda n: pltpu.VMEM((n, TQ, LANES), F32)
    batch = lambda wd: pl.BlockSpec((None, T, wd), lambda b: (b, 0, 0))
    return pl.pallas_call(
        kern,
        grid=(B,),
        in_specs=[batch(3 * GROUP_W), batch(LANES), pl.BlockSpec((1, LANES), lambda b: (0, 0)), batch(3 * GROUP_W)],
        out_specs=[batch(GROUP_W), batch(GROUP_W)],
        out_shape=[jax.ShapeDtypeStruct((B, T, GROUP_W), BF16)] * 2,
        scratch_shapes=[pltpu.VMEM((T, LANES), F32), slot(), slot(), slot(), state(GROUP_HEADS), state(GROUP_HEADS),
                        slot(), slot(), slot(), state(GROUP_HEADS), state(GROUP_HEADS // 2)],
        compiler_params=pltpu.CompilerParams(dimension_semantics=("arbitrary",),
                                             vmem_limit_bytes=VMEM_LIMIT),
        name="fox_stickbreak",
    )(g_fx, g_sm, fb, g_sb)


def _hgrn_decay_matrix(C):
    nl = C.bit_length() - 1
    t = _iota((C, C), 0)
    u = _iota((C, C), 1)
    blocks = [u <= t, u > t]
    q_blocks, k_blocks = [], []
    for l in range(nl):
        m = C >> (l + 1)
        bnd = (t & ~(2 * m - 1)) + (m - 1)
        q_blocks.append((u > bnd) & (u <= t))
        k_blocks.append((u > t) & (u <= bnd))
    d = jnp.concatenate([jnp.where(b, 1.0, 0.0) for b in blocks + q_blocks + k_blocks], axis=0).astype(BF16)
    return jnp.concatenate([d, d, d], axis=1)


def _hgrn_kernel(g_ref, lb_ref, o_ref, st_ref, ex_ref, *, T, C, NCH, layer):
    H = GROUP_HEADS
    nl = C.bit_length() - 1
    lg = lb_ref[...]
    e = jnp.exp(lg - jnp.max(lg, axis=0, keepdims=True))
    sm = e / jnp.sum(e, axis=0, keepdims=True)
    lb = jnp.sum(sm[0:layer + 1, :], axis=0, keepdims=True) - sm[0:1, :]

    dmat = _hgrn_decay_matrix(C)
    lane_head4 = _div(_iota((H * C, GROUP_W), 1), HEAD_DIM)
    row_head4 = _div(_iota((H * C, GROUP_W), 0), C)
    head_sel = lane_head4 == row_head4
    tq_c = _iota((C, GROUP_W), 0)
    t4 = _iota((C, H * C), 0)
    s4 = _iota((C, H * C), 1) & (C - 1)
    bd_mask =_div(_iota((GROUP_W, GROUP_W), 0), HEAD_DIM) == _div(_iota((GROUP_W, GROUP_W), 1), HEAD_DIM)
    st_ref[...] = jnp.zeros((GROUP_W, GROUP_W), F32)

    def decays(r, slot, n):
        g = jnp.log(lb + (1.0 - lb) * _sigmoid(g_ref[r, GROUP_W:2 * GROUP_W]))
        g1, g2, g3 = _split3(g)
        ex_ref[slot, n] = _nn(dmat, jnp.concatenate([g1.astype(BF16), g2.astype(BF16), g3.astype(BF16)], axis=0))

    def intra(r, slot, n):
        ex = ex_ref.at[slot, n]
        qraw = g_ref[r, 0:GROUP_W]
        f = g_ref[r, GROUP_W:2 * GROUP_W]
        v = g_ref[r, 2 * GROUP_W:3 * GROUP_W].astype(BF16)
        q = qraw * _sigmoid(qraw)
        k = (1.0 - lb) * (1.0 - _sigmoid(f))
        b = ex[0:C]

        a = jnp.zeros((C, H * C), F32)
        for l in range(nl + 1):
            if l < nl:
                m = C >> (l + 1)
                dq = ex[(2 + l) * C:(3 + l) * C]
                dk = ex[(2 + nl + l) * C:(3 + nl + l) * C]
                qs = q * jnp.where((tq_c & m) != 0, jnp.exp(dq), 0.0)
                ks = k * jnp.where((tq_c & m) == 0, jnp.exp(dk), 0.0)
                pair = (t4 & ~(2 * m - 1)) == (s4 & ~(2 * m - 1))
            else:
                qs, ks = q, k
                pair = t4 == s4
            ksb = ks.astype(BF16)
            kst = jnp.where(head_sel, jnp.concatenate([ksb] * H, axis=0), jnp.zeros((), BF16))
            a = a + jnp.where(pair, _nt(qs.astype(BF16), kst), 0.0)
        vbd = jnp.where(head_sel, jnp.concatenate([v] * H, axis=0), jnp.zeros((), BF16))
        o_intra = _nn(a.astype(BF16), vbd)
        qb =(q * jnp.exp(b)).astype(BF16)
        kd = (k * jnp.exp(ex[C:2 * C])).astype(BF16)
        upd = jnp.where(bd_mask, _tn(v, kd), 0.0)
        return o_intra, qb, upd, jnp.exp(b[C - 1:C, :])

    NIT = T // (C * NCH)
    for n in range(NCH):
        decays(_rows(n, C), 0, n)

    def body(ci, _):
        slot = ci & 1
        nxt = jnp.minimum(ci + 1, NIT - 1)
        for n in range(NCH):
            decays(_rows(nxt * NCH + n, C), 1 - slot, n)
        rs = [_rows(ci * NCH + n, C) for n in range(NCH)]
        parts = [intra(r, slot, n) for n, r in enumerate(rs)]
        st = st_ref[...]
        for r, (o_intra, qb, upd, decay) in zip(rs, parts):
            o_ref[r, :] = (o_intra + _nt(qb, st.astype(BF16))).astype(BF16)
            st = st * decay + upd
        st_ref[...] = st
        return 0

    lax.fori_loop(0, NIT, body, 0)


def _hgrn2(g_hg, lb_logits, layer, B, T):
    depth = lb_logits.shape[0]
    NCH = 4
    kern = functools.partial(_hgrn_kernel, T=T, C=HG_CHUNK, NCH=NCH, layer=layer)
    return pl.pallas_call(
        kern,
        grid=(B,),
        in_specs=[pl.BlockSpec((None, T, 3 * GROUP_W), lambda b: (b, 0, 0)),
                  pl.BlockSpec((depth, GROUP_W), lambda b: (0, 0))],
        out_specs=pl.BlockSpec((None, T, GROUP_W), lambda b: (b, 0, 0)),
        out_shape=jax.ShapeDtypeStruct((B, T, GROUP_W), BF16),
        scratch_shapes=[pltpu.VMEM((GROUP_W, GROUP_W), F32),
                        pltpu.VMEM((2, NCH, 2 * HG_CHUNK.bit_length() * HG_CHUNK, GROUP_W), F32)],
        compiler_params=pltpu.CompilerParams(dimension_semantics=("arbitrary",),
                                             vmem_limit_bytes=VMEM_LIMIT),
        name="hgrn2",
    )(g_hg, lb_logits)


def _rope(x, cos, sin, lane):
    swapped = jnp.where((lane & (HEAD_DIM - 1)) < ROT_DIM // 2,
                        pltpu.roll(x, LANES - ROT_DIM // 2, 1), pltpu.roll(x, ROT_DIM // 2, 1))
    return x * cos + swapped * sin


def _nsa_kernel(g_ref, sm_ref, cosq_ref, sinq_ref, cosk_ref, sink_ref, cosc_ref, sinc_ref,
                pe_ref, w1_ref, w2_ref, o_ref,
                cv_ref, ablk_ref, qa_ref, kcmp_ref, vcmp_ref, ksel_ref, vsel_ref, kwin_ref, vwin_ref,
                m_ref, acc_ref, score_ref, *, T, TQ, TK):
    H = GROUP_HEADS
    RB = 256
    NC = T // NSA_CMP_STRIDE
    NSEL = T // NSA_SEL_BLOCK
    TOPN = min(NSA_TOP_N, NSEL)
    WT = NSA_WINDOW // TK
    lane = _iota((RB, LANES), 1)
    grow = _iota((RB, LANES), 0)

    def build_body(i, _):
        r = _rows(i, RB)
        cosq, sinq = cosq_ref[r, :], sinq_ref[r, :]
        cosk, sink = cosk_ref[r, :], sink_ref[r, :]
        for j in range(2):
            xr = _rope(g_ref[r, j * LANES:(j + 1) * LANES], cosq, sinq, lane) * SCALE_LOG2
            for h in (2 * j, 2 * j + 1):
                qa_ref[h, r, :] = jnp.where(lane < 64, _head_to_slot(xr, h), 0.0).astype(BF16)
        cv_ref[r, :] = g_ref[r, GROUP_W:GROUP_W + LANES]
        ksvs = g_ref[r, GROUP_W + LANES:GROUP_W + 2 * LANES]
        kwvw = g_ref[r, GROUP_W + 2 * LANES:GROUP_W + 3 * LANES]
        blk = _div(i * RB + grow, NSA_SEL_BLOCK)
        onehot = jnp.where((lane >= 64) & (lane - 64 == blk), 1.0, 0.0)
        ksel_ref[r, :] = jnp.where(lane < 64, _rope(ksvs, cosk, sink, lane), onehot).astype(BF16)
        vsel_ref[r, :] = jnp.where(lane < 64, pltpu.roll(ksvs, 64, 1), 1.0).astype(BF16)
        kwin_ref[r, :] = jnp.where(lane < 64, _rope(kwvw, cosk, sink, lane), 0.0).astype(BF16)
        vwin_ref[r, :] = jnp.where(lane < 64, pltpu.roll(kwvw, 64, 1), 1.0).astype(BF16)
        return 0

    lax.fori_loop(0, T // RB, build_body, 0)
    cv_ref[T:T + 2 * NSA_CMP_STRIDE, :] = jnp.zeros((2 * NSA_CMP_STRIDE, LANES), F32)

    for l in range(NSA_CMP_LEN):
        blk_l = cv_ref[pl.ds(l, NC, stride=NSA_CMP_STRIDE), :] + pe_ref[l:l + 1, :]
        ablk_ref[:, l * LANES:(l + 1) * LANES] = blk_l.astype(BF16)
    hid = _nn(ablk_ref[...], w1_ref[...])
    hid = hid * _sigmoid(hid)
    kv = _nn(hid.astype(BF16), w2_ref[...])
    lane_c = _iota((NC, LANES), 1)
    kcmp_ref[...] = jnp.where(lane_c < 64, _rope(kv, cosc_ref[...], sinc_ref[...], lane_c), 0.0).astype(BF16)
    vcmp_ref[...] = jnp.where(lane_c < 64, pltpu.roll(kv, 64, 1), 0.0).astype(BF16)

    on = _iota((NSEL, 2 * NC), 1) & (NC - 1)
    oj = _iota((NSEL, 2 * NC), 0)
    ovt = jnp.clip(jnp.minimum(on * NSA_CMP_STRIDE + NSA_CMP_LEN, oj * NSA_SEL_BLOCK + NSA_SEL_BLOCK)
                   - jnp.maximum(on * NSA_CMP_STRIDE, oj * NSA_SEL_BLOCK), 0, None).astype(F32) / NSA_CMP_LEN
    ovt = jnp.where(on < NC - 1, ovt, 0.0).astype(BF16)
    jrow = _iota((NSEL, TQ), 0)
    tcol = _iota((NSEL, TQ), 1)

    gr = _iota((2 * LANES, 3 * GROUP_W), 0) & (LANES - 1)
    gc = _iota((2 * LANES, 3 * GROUP_W), 1)
    gate_w = jnp.where(gr == GROUP_HEADS + _div(gc, GROUP_W) * GROUP_HEADS + _div(gc & (GROUP_W - 1), HEAD_DIM),
                       1.0, 0.0).astype(BF16)

    row4 = _iota((H * TQ, TK), 0) & (TQ - 1)
    col4 = _iota((H * TQ, TK), 1)
    lane4 = _iota((H * TQ, LANES), 1)
    lane_q = _iota((TQ, LANES), 1)
    row_q = _iota((TQ, LANES), 0)
    cmp_end = _iota((H * TQ, NC), 1) * NSA_CMP_STRIDE + (NSA_CMP_LEN - 1)
    rowc = _iota((H * TQ, NC), 0) & (TQ - 1)

    def q_body(i, _):
        rq = _rows(i, TQ)
        t0 = i * TQ
        qst = jnp.concatenate([qa_ref[h, rq, :] for h in range(H)], axis=0)

        valid = cmp_end <= (t0 + rowc)
        s = jnp.where(valid, _nt(qst, kcmp_ref[...]), NEG_BIG)
        p = jnp.where(valid, jnp.exp2(s - jnp.max(s, axis=-1, keepdims=True)), 0.0)
        den = jnp.sum(p, axis=-1, keepdims=True)
        p = p / jnp.where(den > 0, den, 1.0)
        o_cmp = _nn(p.astype(BF16), vcmp_ref[...])
        psum = p[0:TQ]
        for h in range(1, H):
            psum = psum + p[h * TQ:(h + 1) * TQ]
        hi, lo = _split2(psum)
        imp = _nt(ovt, jnp.concatenate([hi, lo], axis=1))

        qblk = _div(t0 + tcol, NSA_SEL_BLOCK)
        forced = (jrow == 0) | (jrow == qblk) | (jrow == qblk - 1)
        score = jnp.where(forced, NSA_FORCE, imp)
        score_ref[...] = jnp.where(jrow <= qblk, score, -NSA_FORCE)
        score = score_ref[...]
        rank = jnp.zeros((NSEL, TQ), F32)
        for ii in range(NSEL):
            ci = score_ref[ii:ii + 1, :]
            rank = rank + jnp.where((ci > score) | ((ci == score) & (ii < jrow)), 1.0, 0.0)
        sel = (rank < TOPN) & (jrow <= qblk)
        bias = jnp.where(sel, 0.0, SEL_OFF)
        aug = jnp.concatenate([jnp.zeros((HEAD_DIM, TQ), F32), bias,
                               jnp.zeros((LANES - HEAD_DIM - NSEL, TQ), F32)], axis=0).T.astype(BF16)
        qsel = jnp.where(lane4 < 64, qst, jnp.concatenate([aug] * H, axis=0))

        def att_step(j, _, slot, q_in, k_ref, v_ref, mode):
            rk = _rows(j, TK)
            s = _nt(q_in, k_ref[rk, :])
            if mode is not None:
                kpos = j * TK + col4
                qpos = t0 + row4
                msk = (kpos <= qpos) if mode == "causal" else (kpos > qpos - NSA_WINDOW)
                s = jnp.where(msk, s, NEG_BIG)
            m = m_ref[slot]
            m_new = jnp.maximum(m, jnp.max(s, axis=-1, keepdims=True))
            p = jnp.exp2(jnp.concatenate([s[:, c0:c0 + LANES] - m_new for c0 in range(0, TK, LANES)], axis=1))
            if mode is not None:
                p = jnp.where(msk, p, 0.0)
            acc_ref[slot] = jnp.exp2(m - m_new) * acc_ref[slot] + _nn(p.astype(BF16), v_ref[rk, :])
            m_ref[slot] = m_new
            return 0

        for slot in range(2):
            m_ref[slot] = jnp.full((H * TQ, LANES), NEG_BIG, F32)
            acc_ref[slot] = jnp.zeros((H * TQ, LANES), F32)
        sel_step = functools.partial(att_step, slot=0, q_in=qsel, k_ref=ksel_ref, v_ref=vsel_ref)
        def sel_pair(jj, _):
            sel_step(2 * jj, 0, mode=None)
            sel_step(2 * jj + 1, 0, mode=None)
            return 0

        lax.fori_loop(0, i >> 1, sel_pair, 0)

        @pl.when((i & 1) == 1)
        def _():
            sel_step(i - 1, 0, mode=None)

        sel_step(i, 0, mode="causal")
        win_step = functools.partial(att_step, slot=1, q_in=qst, k_ref=kwin_ref, v_ref=vwin_ref)
        for w in range(WT, 0, -1):
            @pl.when(i >= w)
            def _(w=w):
                win_step(i - w, 0, mode="window" if w == WT else None)

        win_step(i, 0, mode="causal")

        hi, lo = _split2(_sigmoid(sm_ref[rq, :]))
        gmap = _nn(jnp.concatenate([hi, lo], axis=1), gate_w)
        low = lane_q < 64
        for j in range(H // 2):
            re = slice(2 * j * TQ, (2 * j + 1) * TQ)
            ro = slice((2 * j + 1) * TQ, (2 * j + 2) * TQ)
            out = gmap[:, j * LANES:(j + 1) * LANES] * jnp.where(low, o_cmp[re], pltpu.roll(o_cmp[ro], 64, 1))
            for slot in range(2):
                ae, ao = acc_ref[slot, re, :], acc_ref[slot, ro, :]
                num = jnp.where(low, ae, pltpu.roll(ao, 64, 1))
                den = jnp.where(low, pltpu.roll(ae, 64, 1), ao)
                c0 = (slot + 1) * GROUP_W + j * LANES
                out = out + gmap[:, c0:c0 + LANES] * (num / den)
            o_ref[rq, j * LANES:(j + 1) * LANES] = out.astype(BF16)
        return 0

    lax.fori_loop(0, T // TQ, q_body, 0)


def _rope_tables(pos):
    half = ROT_DIM // 2
    inv_freq = ROPE_THETA ** (-(jnp.arange(half, dtype=F32) * 2.0 / ROT_DIM))
    ang = pos.astype(F32)[:, None] * inv_freq[None, :]
    cos, sin = jnp.cos(ang), jnp.sin(ang)
    n = pos.shape[0]
    c64 = jnp.concatenate([cos, cos, jnp.ones((n, HEAD_DIM - ROT_DIM), F32)], axis=1)
    s64 = jnp.concatenate([-sin, sin, jnp.zeros((n, HEAD_DIM - ROT_DIM), F32)], axis=1)
    return c64, s64


def _nsa(g_ns, g_sm, pe, w1, w2, B, T):
    TQ = TK = 256
    NC = T // NSA_CMP_STRIDE
    c64, s64 = _rope_tables(jnp.arange(T))
    cosq, sinq = jnp.tile(c64, (1, 2)), jnp.tile(s64, (1, 2))
    cosk = jnp.concatenate([c64, jnp.ones((T, HEAD_DIM), F32)], axis=1)
    sink = jnp.concatenate([s64, jnp.zeros((T, HEAD_DIM), F32)], axis=1)
    cc, sc = _rope_tables(jnp.arange(NC) * NSA_CMP_STRIDE + NSA_CMP_LEN - 1)
    cosc = jnp.concatenate([cc, jnp.ones((NC, HEAD_DIM), F32)], axis=1)
    sinc = jnp.concatenate([sc, jnp.zeros((NC, HEAD_DIM), F32)], axis=1)
    kern = functools.partial(_nsa_kernel, T=T, TQ=TQ, TK=TK)
    full = lambda shape: pl.BlockSpec(shape, lambda b: (0,) * len(shape))
    ns_w = GROUP_W + 3 * LANES
    return pl.pallas_call(
        kern,
        grid=(B,),
        in_specs=[pl.BlockSpec((None, T, ns_w), lambda b: (b, 0, 0)),
                  pl.BlockSpec((None, T, LANES), lambda b: (b, 0, 0)),
                  full((T, LANES)), full((T, LANES)), full((T, LANES)), full((T, LANES)),
                  full((NC, LANES)), full((NC, LANES)),
                  full((NSA_CMP_LEN, LANES)), full((NSA_CMP_LEN * LANES, LANES)), full((LANES, LANES))],
        out_specs=pl.BlockSpec((None, T, GROUP_W), lambda b: (b, 0, 0)),
        out_shape=jax.ShapeDtypeStruct((B, T, GROUP_W), BF16),
        scratch_shapes=[pltpu.VMEM((T + 2 * NSA_CMP_STRIDE, LANES), F32),
                        pltpu.VMEM((NC, NSA_CMP_LEN * LANES), BF16),
                        pltpu.VMEM((GROUP_HEADS, T, LANES), BF16),
                        pltpu.VMEM((NC, LANES), BF16), pltpu.VMEM((NC, LANES), BF16)]
        + [pltpu.VMEM((T, LANES), BF16) for _ in range(4)]
        + [pltpu.VMEM((2, GROUP_HEADS * TQ, LANES), F32) for _ in range(2)]
        + [pltpu.VMEM((T // NSA_SEL_BLOCK, TQ), F32)],
        compiler_params=pltpu.CompilerParams(dimension_semantics=("arbitrary",),
                                             vmem_limit_bytes=VMEM_LIMIT),
        name="nsa",
    )(g_ns, g_sm, cosq, sinq, cosk, sink, cosc, sinc, pe, w1, w2)


def _out_kernel(y_hg, y_fx, y_sb, y_ns, gate_ref, x_ref, gn_ref, w_ref, fg_ref, o_ref, *, final):
    bd = jnp.where(_div(_iota((2 * LANES, LANES), 0) & (LANES - 1), HEAD_DIM)
                   == _div(_iota((2 * LANES, LANES), 1), HEAD_DIM), 1.0, 0.0).astype(BF16)
    parts = []
    for ref in (y_hg, y_fx, y_sb, y_ns):
        y = ref[...].astype(F32)
        for j in range(GROUP_W // LANES):
            yy = y[:, j * LANES:(j + 1) * LANES]
            hi, lo = _split2(yy * yy)
            ms = _nn(jnp.concatenate([hi, lo], axis=1), bd) * (1.0 / HEAD_DIM)
            parts.append(yy * lax.rsqrt(ms + EPS))
    gt = gate_ref[...].astype(F32)
    z = jnp.concatenate(parts, axis=1) * gn_ref[...] * (gt * _sigmoid(gt))
    out = x_ref[...] + _nn(z.astype(BF16), w_ref[...])
    if final:
        out = out * lax.rsqrt(jnp.mean(out * out, axis=-1, keepdims=True) + EPS) * fg_ref[...]
    o_ref[...] = out


def _out_project(ys, gate, x2, gn, w_out, fg, final, tm):
    n = x2.shape[0]
    kern = functools.partial(_out_kernel, final=final)
    row = lambda wd: pl.BlockSpec((tm, wd), lambda i: (i, 0))
    return pl.pallas_call(
        kern,
        grid=(n // tm,),
        in_specs=[row(GROUP_W)] * 4 + [row(D_MODEL), row(D_MODEL),
                                       pl.BlockSpec((1, D_MODEL), lambda i: (0, 0)),
                                       pl.BlockSpec((D_MODEL, D_MODEL), lambda i: (0, 0)),
                                       pl.BlockSpec((1, D_MODEL), lambda i: (0, 0))],
        out_specs=row(D_MODEL),
        out_shape=jax.ShapeDtypeStruct((n, D_MODEL), F32),
        compiler_params=pltpu.CompilerParams(dimension_semantics=("arbitrary",),
                                             vmem_limit_bytes=VMEM_LIMIT),
        name="outproj",
    )(*ys, gate, x2, gn.reshape(1, D_MODEL), w_out, fg.reshape(1, D_MODEL))


def _pack_in_weights(w):
    o = IN_OFFS
    sm = jnp.concatenate([w[:, o[6]:o[7]], w[:, o[17]:o[18]],
                          jnp.zeros((D_MODEL, LANES - GROUP_HEADS - 3 * GROUP_HEADS), w.dtype)], axis=1)
    groups = [w[:, o[0]:o[3]], w[:, o[3]:o[6]], w[:, o[7]:o[10]], w[:, o[10]:o[17]], w[:, o[18]:o[19]], sm]
    return [g.astype(BF16) for g in groups]


def _pack_cmp_weights(pe_k, w1_k, w2_k, pe_v, w1_v, w2_v):
    L, d = NSA_CMP_LEN, HEAD_DIM
    z = jnp.zeros((L, d, d), F32)
    w1 = jnp.concatenate([jnp.concatenate([w1_k.reshape(L, d, d), z], axis=2),
                          jnp.concatenate([z, w1_v.reshape(L, d, d)], axis=2)], axis=1).reshape(L * 2 * d, 2 * d)
    z2 = jnp.zeros((d, d), F32)
    w2 = jnp.concatenate([jnp.concatenate([w2_k, z2], axis=1), jnp.concatenate([z2, w2_v], axis=1)], axis=0)
    return jnp.concatenate([pe_k, pe_v], axis=1), w1.astype(BF16), w2.astype(BF16)


def kernel(x, norm_g, w_in, hgrn_lb_logits, fox_fb, nsa_cmp_pe_k, nsa_cmp_w1_k, nsa_cmp_w2_k,
           nsa_cmp_pe_v, nsa_cmp_w1_v, nsa_cmp_w2_v, out_norm_g, w_out, final_norm_g):
    B, T, D = x.shape
    depth = w_in.shape[0]
    x2 = x.reshape(B * T, D)
    for l in range(depth):
        g_hg, g_fx, g_sb, g_ns, g_gate, g_sm = _project(x2, norm_g[l], _pack_in_weights(w_in[l]), 512)
        g_sm3 = g_sm.reshape(B, T, LANES)
        fb = jnp.concatenate([fox_fb[l], jnp.zeros((LANES - GROUP_HEADS,), F32)]).reshape(1, LANES)
        pe, w1, w2 = _pack_cmp_weights(nsa_cmp_pe_k[l], nsa_cmp_w1_k[l], nsa_cmp_w2_k[l],
                                       nsa_cmp_pe_v[l], nsa_cmp_w1_v[l], nsa_cmp_w2_v[l])
        y_hg = _hgrn2(g_hg.reshape(B, T, -1), hgrn_lb_logits, l, B, T)
        y_fx, y_sb = _fox_stick_breaking(g_fx.reshape(B, T, -1), g_sm3, fb, g_sb.reshape(B, T, -1), B, T)
        y_ns = _nsa(g_ns.reshape(B, T, -1), g_sm3, pe, w1, w2, B, T)
        ys = [y.reshape(B * T, GROUP_W) for y in (y_hg, y_fx, y_sb, y_ns)]
        x2 = _out_project(ys, g_gate, x2, out_norm_g[l], w_out[l].astype(BF16), final_norm_g,
                          l == depth - 1, 512)
    return x2.reshape(B, T, D)
```

```python
import functools

import numpy as np
import jax
import jax.numpy as jnp
from jax import lax
from jax.experimental import pallas as pl
from jax.experimental.pallas import tpu as pltpu

F32 = jnp.float32
BF16 = jnp.bfloat16

D_MODEL = 1024
HEAD_DIM = 64
GROUP_W = 256
GROUP_HEADS = 4
ATTN_SCALE = HEAD_DIM ** -0.5
LOG2E = 1.4426950408889634
SCALE_LOG2 = ATTN_SCALE * LOG2E
ROPE_THETA = 500000.0
ROT_DIM = HEAD_DIM // 4
HG_CHUNK = 64
NSA_CMP_LEN = 32
NSA_CMP_STRIDE = 16
NSA_SEL_BLOCK = 64
NSA_TOP_N = 16
NSA_WINDOW = 512
NSA_FORCE = 1.0e4
NEG_BIG = -1.0e30
SEL_OFF = -(2.0 ** 100)
EPS = 1e-6

LANES = 128
VMEM_LIMIT = 56 * 1024 * 1024

IN_WIDTHS = (GROUP_W,) * 3 + (GROUP_W,) * 3 + (GROUP_HEADS,) + (GROUP_W,) * 3 + (GROUP_W,) \
    + (HEAD_DIM,) * 6 + (3 * GROUP_HEADS,) + (D_MODEL,)
IN_OFFS = np.concatenate([[0], np.cumsum(IN_WIDTHS)]).tolist()
GATE_GROUP = 4


def _nn(a, b):
    return jnp.dot(a, b, preferred_element_type=F32)


def _nt(a, b):
    return lax.dot_general(a, b, (((1,), (1,)), ((), ())), preferred_element_type=F32)


def _tn(a, b):
    return lax.dot_general(a, b, (((0,), (0,)), ((), ())), preferred_element_type=F32)


def _iota(shape, dim):
    return lax.broadcasted_iota(jnp.int32, shape, dim)


def _split2(x):
    hi = x.astype(BF16)
    lo = (x - hi.astype(F32)).astype(BF16)
    return hi, lo


def _split3(x):
    p1 = x.astype(BF16).astype(F32)
    r = x - p1
    p2 = r.astype(BF16).astype(F32)
    p3 = (r - p2).astype(BF16).astype(F32)
    return p1, p2, p3


def _sigmoid(x):
    return 1.0 / (1.0 + jnp.exp(-x))


def _log_sigmoid(x):
    return jnp.minimum(x, 0.0) - jnp.log(1.0 + jnp.exp(-jnp.abs(x)))


def _div(x, n):
    assert n & (n - 1) == 0
    return x >> (n.bit_length() - 1)


def _rows(i, n):
    return pl.ds(pl.multiple_of(i * n, n), n)


def _head_to_slot(x, h):
    return x if h % 2 == 0 else pltpu.roll(x, HEAD_DIM, 1)


def _proj_kernel(x_ref, g_ref, w_hg, w_fx, w_sb, w_ns, w_gate, w_sm,
                 o_hg, o_fx, o_sb, o_ns, o_gate, o_sm):
    x = x_ref[...]
    h = x * lax.rsqrt(jnp.mean(x * x, axis=-1, keepdims=True) + EPS) * g_ref[...]
    hb = h.astype(BF16)
    for w, o in ((w_hg, o_hg), (w_fx, o_fx), (w_sb, o_sb), (w_ns, o_ns), (w_gate, o_gate), (w_sm, o_sm)):
        o[...] = _nn(hb, w[...]).astype(o.dtype)


def _project(x2, norm_g, ws, tm):
    n = x2.shape[0]
    widths = [w.shape[1] for w in ws]
    return pl.pallas_call(
        _proj_kernel,
        grid=(n // tm,),
        in_specs=[pl.BlockSpec((tm, D_MODEL), lambda i: (i, 0)),
                  pl.BlockSpec((1, D_MODEL), lambda i: (0, 0))]
        + [pl.BlockSpec((D_MODEL, wd), lambda i: (0, 0)) for wd in widths],
        out_specs=[pl.BlockSpec((tm, wd), lambda i: (i, 0)) for wd in widths],
        out_shape=[jax.ShapeDtypeStruct((n, wd), BF16 if gi == GATE_GROUP else F32) for gi, wd in enumerate(widths)],
        compiler_params=pltpu.CompilerParams(dimension_semantics=("arbitrary",),
                                             vmem_limit_bytes=VMEM_LIMIT),
        name="proj",
    )(x2, norm_g.reshape(1, D_MODEL), *ws)


def _fox_program(g_ref, sm_ref, fb_ref, o_ref, cum_ref, qa_ref, ka_ref, va_ref, m_ref, acc_ref, *, T, TQ, TK):
    H = GROUP_HEADS
    RB = 256
    NB = T // LANES
    tri = jnp.where(_iota((LANES, LANES), 0) >= _iota((LANES, LANES), 1), 1.0, 0.0).astype(BF16)
    fb = fb_ref[...]

    local = []
    for i in range(NB):
        p1, p2, p3 = _split3(_log_sigmoid(sm_ref[i * LANES:(i + 1) * LANES, :] + fb))
        local.append(_nn(tri, p1.astype(BF16)) + _nn(tri, p2.astype(BF16)) + _nn(tri, p3.astype(BF16)))
    carry = jnp.zeros((1, LANES), F32)
    for i in range(NB):
        c = local[i] + carry
        cum_ref[i * LANES:(i + 1) * LANES, :] = c
        carry = c[LANES - 1:LANES, :]

    lane = _iota((RB, LANES), 1)
    upper = lane >= HEAD_DIM

    def build_body(i, _):
        r = _rows(i, RB)
        cum = cum_ref[r, :]
        for h in range(H):
            j = h // 2
            own = upper if h % 2 else ~upper
            a0 = 0 if h % 2 else HEAD_DIM
            q = g_ref[r, j * LANES:(j + 1) * LANES] * SCALE_LOG2
            k = g_ref[r, GROUP_W + j * LANES:GROUP_W + (j + 1) * LANES]
            v = g_ref[r, 2 * GROUP_W + j * LANES:2 * GROUP_W + (j + 1) * LANES]
            qa = jnp.where(own, q, 0.0)
            ka = jnp.where(own, k, 0.0)
            for p, c in enumerate(_split3(jnp.broadcast_to(cum[:, h:h + 1], (RB, LANES)) * LOG2E)):
                qa = jnp.where(lane == a0 + p, c, qa)
                qa = jnp.where(lane == a0 + 3 + p, 1.0, qa)
                ka = jnp.where(lane == a0 + p, 1.0, ka)
                ka = jnp.where(lane == a0 + 3 + p, -c, ka)
            qa_ref[h, r, :] = qa.astype(BF16)
            ka_ref[h, r, :] = ka.astype(BF16)
            va_ref[h, r, :] = jnp.where(own, v, 1.0).astype(BF16)
        return 0

    lax.fori_loop(0, T // RB, build_body, 0)

    NSUB = TQ // TK
    upper_q =_iota((TQ, LANES), 1) >= HEAD_DIM

    def start(i):
        for h in range(H):
            m_ref[h] = jnp.full((TQ, LANES), NEG_BIG, F32)
            acc_ref[h] = jnp.zeros((TQ, LANES), F32)
        return [qa_ref[h, _rows(i, TQ), :] for h in range(H)]

    def k_step(qas, i, j, masked, r0=0):
        rk = _rows(j, TK)
        rs = slice(r0, TQ)
        for h in range(H):
            s = _nt(qas[h][rs], ka_ref[h, rk, :])
            if masked:
                off = _iota((TQ - r0, TK), 1) - _iota((TQ - r0, TK), 0)
                s = jnp.where(off <= i * TQ + r0 - j * TK, s, NEG_BIG)
            m = m_ref[h, rs, :]
            m_new = jnp.maximum(m, jnp.max(s, axis=-1, keepdims=True))
            p = jnp.exp2(jnp.concatenate([s[:, c0:c0 + LANES] - m_new for c0 in range(0, TK, LANES)], axis=1))
            acc_ref[h, rs, :] = jnp.exp2(m - m_new) * acc_ref[h, rs, :] + _nn(p.astype(BF16), va_ref[h, rk, :])
            m_ref[h, rs, :] = m_new

    def diag(qas, i):
        for d in range(NSUB):
            k_step(qas, i, i * NSUB + d, True, r0=d * TK)

    def pair(qas, i, jj):
        for d in range(NSUB):
            k_step(qas, i, jj * NSUB + d, False)

    def finish(i):
        for j in range(H // 2):
            ae, ao = acc_ref[2 * j], acc_ref[2 * j + 1]
            den = pltpu.roll(jnp.where(upper_q, ae, ao), HEAD_DIM, 1)
            o_ref[_rows(i, TQ), j * LANES:(j + 1) * LANES] = (jnp.where(upper_q, ao, ae) / den).astype(BF16)

    return start, diag, pair, finish


def _sb_program(g_ref, o_ref, qa_ref, ka_ref, va_ref, c_ref, acc_ref, *, T, TQ, TK):
    H = GROUP_HEADS
    RB = 256
    SUB = LANES
    NSUB = TQ // TK
    upper = _iota((RB, LANES), 1) >= HEAD_DIM

    def build_body(i, _):
        r = _rows(i, RB)
        for h in range(H):
            j = h // 2
            own = upper if h % 2 else ~upper
            q = g_ref[r, j * LANES:(j + 1) * LANES] * ATTN_SCALE
            k = g_ref[r, GROUP_W + j * LANES:GROUP_W + (j + 1) * LANES]
            v = g_ref[r, 2 * GROUP_W + j * LANES:2 * GROUP_W + (j + 1) * LANES]
            qa_ref[h, r, :] = jnp.where(own, q, 0.0).astype(BF16)
            ka_ref[h, r, :] = jnp.where(own, k, 0.0).astype(BF16)
            va_ref[h, r, :] = jnp.where(own, v, 0.0).astype(BF16)
        return 0

    lax.fori_loop(0, T // RB, build_body, 0)

    wr = _iota((2 * SUB, 2 * SUB), 0) & (SUB - 1)
    wc = _iota((2 * SUB, 2 * SUB), 1)
    suf_w = jnp.where((wc >= SUB) | (wr > wc), 1.0, 0.0).astype(BF16)

    def start(i):
        for h in range(H):
            c_ref[h] = jnp.zeros((TQ, LANES), F32)
        for j in range(H // 2):
            acc_ref[j] = jnp.zeros((TQ, LANES), F32)
        return [qa_ref[h, _rows(i, TQ), :] for h in range(H)]

    def k_step(qas, i, j, masked, r0=0):
        rk = _rows(j, TK)
        rs = slice(r0, TQ)
        for h in range(H):
            z = _nt(qas[h][rs], ka_ref[h, rk, :])
            ls = jnp.minimum(z, 0.0) - jnp.log(1.0 + jnp.exp2(jnp.abs(z) * -LOG2E))
            lom = ls - z
            if masked:
                off = _iota((TQ - r0, TK), 1) - _iota((TQ - r0, TK), 0)
                msk = off < i * TQ + r0 - j * TK
                lom = jnp.where(msk, lom, 0.0)
            c = c_ref[h, rs, :]
            parts = []
            for sb in reversed(range(TK // SUB)):
                hi, lo = _split2(lom[:, sb * SUB:(sb + 1) * SUB])
                cs = _nn(jnp.concatenate([hi, lo], axis=1), suf_w)
                parts.append(ls[:, sb * SUB:(sb + 1) * SUB] + cs[:, :SUB] + c)
                c = c + cs[:, SUB:]
            c_ref[h, rs, :] = c
            a = jnp.exp(jnp.concatenate(parts[::-1], axis=1))
            if masked:
                a = jnp.where(msk, a, 0.0)
            acc_ref[h // 2, rs, :] += _nn(a.astype(BF16), va_ref[h, rk, :])

    def diag(qas, i):
        for d in reversed(range(NSUB)):
            k_step(qas, i, i * NSUB + d, True, r0=d * TK)

    def pair(qas, i, jj):
        for d in range(NSUB):
            k_step(qas, i, (i - jj) * NSUB - 1 - d, False)

    def finish(i):
        for j in range(H // 2):
            o_ref[_rows(i, TQ), j * LANES:(j + 1) * LANES] = acc_ref[j].astype(BF16)

    return start, diag, pair, finish


def _fox_sb_kernel(fx_ref, sm_ref, fb_ref, sb_ref, ofx_ref, osb_ref,
                   cum_ref, fqa, fka, fva, fm, facc, sqa, ska, sva, sc, sacc, *, T, TQ, TK):
    fox = _fox_program(fx_ref, sm_ref, fb_ref, ofx_ref, cum_ref, fqa, fka, fva, fm, facc, T=T, TQ=TQ, TK=TK)
    stb = _sb_program(sb_ref, osb_ref, sqa, ska, sva, sc, sacc, T=T, TQ=TQ, TK=TK)
    progs = (fox, stb)

    def q_body(i, _):
        ctx = [p[0](i) for p in progs]
        for p, qas in zip(progs, ctx):
            p[1](qas, i)

        def k_body(jj, _):
            for p, qas in zip(progs, ctx):
                p[2](qas, i, jj)
            return 0

        lax.fori_loop(0, i, k_body, 0)
        for p in progs:
            p[3](i)
        return 0

    lax.fori_loop(0, T // TQ, q_body, 0)


def _fox_stick_breaking(g_fx, g_sm, fb, g_sb, B, T):
    TQ, TK = 512, 256
    kern = functools.partial(_fox_sb_kernel, T=T, TQ=TQ, TK=TK)
    slot = lambda: pltpu.VMEM((GROUP_HEADS, T, LANES), BF16)
    state = lambda n: pltpu.VMEM((n, TQ, LANES), F32)
    batch = lambda wd: pl.BlockSpec((None, T, wd), lambda b: (b, 0, 0))
    return pl.pallas_call(
        kern,
        grid=(B,),
        in_specs=[batch(3 * GROUP_W), batch(LANES), pl.BlockSpec((1, LANES), lambda b: (0, 0)), batch(3 * GROUP_W)],
        out_specs=[batch(GROUP_W), batch(GROUP_W)],
        out_shape=[jax.ShapeDtypeStruct((B, T, GROUP_W), BF16)] * 2,
        scratch_shapes=[pltpu.VMEM((T, LANES), F32), slot(), slot(), slot(), state(GROUP_HEADS), state(GROUP_HEADS),
                        slot(), slot(), slot(), state(GROUP_HEADS), state(GROUP_HEADS // 2)],
        compiler_params=pltpu.CompilerParams(dimension_semantics=("arbitrary",),
                                             vmem_limit_bytes=VMEM_LIMIT),
        name="fox_stickbreak",
    )(g_fx, g_sm, fb, g_sb)


def _hgrn_decay_matrix(C):
    nl = C.bit_length() - 1
    t = _iota((C, C), 0)
    u = _iota((C, C), 1)
    blocks = [u <= t, u > t]
    q_blocks, k_blocks = [], []
    for l in range(nl):
        m = C >> (l + 1)
        bnd = (t & ~(2 * m - 1)) + (m - 1)
        q_blocks.append((u > bnd) & (u <= t))
        k_blocks.append((u > t) & (u <= bnd))
    d = jnp.concatenate([jnp.where(b, 1.0, 0.0) for b in blocks + q_blocks + k_blocks], axis=0).astype(BF16)
    return jnp.concatenate([d, d, d], axis=1)


def _hgrn_kernel(g_ref, lb_ref, o_ref, st_ref, ex_ref, *, T, C, NCH, layer):
    H = GROUP_HEADS
    nl = C.bit_length() - 1
    lg = lb_ref[...]
    e = jnp.exp(lg - jnp.max(lg, axis=0, keepdims=True))
    sm = e / jnp.sum(e, axis=0, keepdims=True)
    lb = jnp.sum(sm[0:layer + 1, :], axis=0, keepdims=True) - sm[0:1, :]

    dmat = _hgrn_decay_matrix(C)
    lane_head4 = _div(_iota((H * C, GROUP_W), 1), HEAD_DIM)
    row_head4 = _div(_iota((H * C, GROUP_W), 0), C)
    head_sel = lane_head4 == row_head4
    tq_c = _iota((C, GROUP_W), 0)
    t4 = _iota((C, H * C), 0)
    s4 = _iota((C, H * C), 1) & (C - 1)
    bd_mask =_div(_iota((GROUP_W, GROUP_W), 0), HEAD_DIM) == _div(_iota((GROUP_W, GROUP_W), 1), HEAD_DIM)
    st_ref[...] = jnp.zeros((GROUP_W, GROUP_W), F32)

    def decays(r, slot, n):
        g = jnp.log(lb + (1.0 - lb) * _sigmoid(g_ref[r, GROUP_W:2 * GROUP_W]))
        g1, g2, g3 = _split3(g)
        ex_ref[slot, n] = _nn(dmat, jnp.concatenate([g1.astype(BF16), g2.astype(BF16), g3.astype(BF16)], axis=0))

    def intra(r, slot, n):
        ex = ex_ref.at[slot, n]
        qraw = g_ref[r, 0:GROUP_W]
        f = g_ref[r, GROUP_W:2 * GROUP_W]
        v = g_ref[r, 2 * GROUP_W:3 * GROUP_W].astype(BF16)
        q = qraw * _sigmoid(qraw)
        k = (1.0 - lb) * (1.0 - _sigmoid(f))
        b = ex[0:C]

        a = jnp.zeros((C, H * C), F32)
        for l in range(nl + 1):
            if l < nl:
                m = C >> (l + 1)
                dq = ex[(2 + l) * C:(3 + l) * C]
                dk = ex[(2 + nl + l) * C:(3 + nl + l) * C]
                qs = q * jnp.where((tq_c & m) != 0, jnp.exp(dq), 0.0)
                ks = k * jnp.where((tq_c & m) == 0, jnp.exp(dk), 0.0)
                pair = (t4 & ~(2 * m - 1)) == (s4 & ~(2 * m - 1))
            else:
                qs, ks = q, k
                pair = t4 == s4
            ksb = ks.astype(BF16)
            kst = jnp.where(head_sel, jnp.concatenate([ksb] * H, axis=0), jnp.zeros((), BF16))
            a = a + jnp.where(pair, _nt(qs.astype(BF16), kst), 0.0)
        vbd = jnp.where(head_sel, jnp.concatenate([v] * H, axis=0), jnp.zeros((), BF16))
        o_intra = _nn(a.astype(BF16), vbd)
        qb =(q * jnp.exp(b)).astype(BF16)
        kd = (k * jnp.exp(ex[C:2 * C])).astype(BF16)
        upd = jnp.where(bd_mask, _tn(v, kd), 0.0)
        return o_intra, qb, upd, jnp.exp(b[C - 1:C, :])

    NIT = T // (C * NCH)
    for n in range(NCH):
        decays(_rows(n, C), 0, n)

    def body(ci, _):
        slot = ci & 1
        nxt = jnp.minimum(ci + 1, NIT - 1)
        for n in range(NCH):
            decays(_rows(nxt * NCH + n, C), 1 - slot, n)
        rs = [_rows(ci * NCH + n, C) for n in range(NCH)]
        parts = [intra(r, slot, n) for n, r in enumerate(rs)]
        st = st_ref[...]
        for r, (o_intra, qb, upd, decay) in zip(rs, parts):
            o_ref[r, :] = (o_intra + _nt(qb, st.astype(BF16))).astype(BF16)
            st = st * decay + upd
        st_ref[...] = st
        return 0

    lax.fori_loop(0, NIT, body, 0)


def _hgrn2(g_hg, lb_logits, layer, B, T):
    depth = lb_logits.shape[0]
    NCH = 8
    kern = functools.partial(_hgrn_kernel, T=T, C=HG_CHUNK, NCH=NCH, layer=layer)
    return pl.pallas_call(
        kern,
        grid=(B,),
        in_specs=[pl.BlockSpec((None, T, 3 * GROUP_W), lambda b: (b, 0, 0)),
                  pl.BlockSpec((depth, GROUP_W), lambda b: (0, 0))],
        out_specs=pl.BlockSpec((None, T, GROUP_W), lambda b: (b, 0, 0)),
        out_shape=jax.ShapeDtypeStruct((B, T, GROUP_W), BF16),
        scratch_shapes=[pltpu.VMEM((GROUP_W, GROUP_W), F32),
                        pltpu.VMEM((2, NCH, 2 * HG_CHUNK.bit_length() * HG_CHUNK, GROUP_W), F32)],
        compiler_params=pltpu.CompilerParams(dimension_semantics=("arbitrary",),
                                             vmem_limit_bytes=VMEM_LIMIT),
        name="hgrn2",
    )(g_hg, lb_logits)


def _rope(x, cos, sin, lane):
    swapped = jnp.where((lane & (HEAD_DIM - 1)) < ROT_DIM // 2,
                        pltpu.roll(x, LANES - ROT_DIM // 2, 1), pltpu.roll(x, ROT_DIM // 2, 1))
    return x * cos + swapped * sin


def _nsa_kernel(g_ref, sm_ref, cosq_ref, sinq_ref, cosk_ref, sink_ref, cosc_ref, sinc_ref,
                pe_ref, w1_ref, w2_ref, o_ref,
                cv_ref, ablk_ref, qa_ref, kcmp_ref, vcmp_ref, ksel_ref, vsel_ref, kwin_ref, vwin_ref,
                m_ref, acc_ref, score_ref, *, T, TQ, TK):
    H = GROUP_HEADS
    RB = 256
    NC = T // NSA_CMP_STRIDE
    NSEL = T // NSA_SEL_BLOCK
    TOPN = min(NSA_TOP_N, NSEL)
    WT = NSA_WINDOW // TK
    SEL_U = 4
    lane = _iota((RB, LANES), 1)
    grow = _iota((RB, LANES), 0)

    def build_body(i, _):
        r = _rows(i, RB)
        cosq, sinq = cosq_ref[r, :], sinq_ref[r, :]
        cosk, sink = cosk_ref[r, :], sink_ref[r, :]
        for j in range(2):
            xr = _rope(g_ref[r, j * LANES:(j + 1) * LANES], cosq, sinq, lane) * SCALE_LOG2
            for h in (2 * j, 2 * j + 1):
                qa_ref[h, r, :] = jnp.where(lane < 64, _head_to_slot(xr, h), 0.0).astype(BF16)
        cv_ref[r, :] = g_ref[r, GROUP_W:GROUP_W + LANES]
        ksvs = g_ref[r, GROUP_W + LANES:GROUP_W + 2 * LANES]
        kwvw = g_ref[r, GROUP_W + 2 * LANES:GROUP_W + 3 * LANES]
        blk = _div(i * RB + grow, NSA_SEL_BLOCK)
        onehot = jnp.where((lane >= 64) & (lane - 64 == blk), 1.0, 0.0)
        ksel_ref[r, :] = jnp.where(lane < 64, _rope(ksvs, cosk, sink, lane), onehot).astype(BF16)
        vsel_ref[r, :] = jnp.where(lane < 64, pltpu.roll(ksvs, 64, 1), 1.0).astype(BF16)
        kwin_ref[r, :] = jnp.where(lane < 64, _rope(kwvw, cosk, sink, lane), 0.0).astype(BF16)
        vwin_ref[r, :] = jnp.where(lane < 64, pltpu.roll(kwvw, 64, 1), 1.0).astype(BF16)
        return 0

    lax.fori_loop(0, T // RB, build_body, 0)
    cv_ref[T:T + 2 * NSA_CMP_STRIDE, :] = jnp.zeros((2 * NSA_CMP_STRIDE, LANES), F32)

    for l in range(NSA_CMP_LEN):
        blk_l = cv_ref[pl.ds(l, NC, stride=NSA_CMP_STRIDE), :] + pe_ref[l:l + 1, :]
        ablk_ref[:, l * LANES:(l + 1) * LANES] = blk_l.astype(BF16)
    hid = _nn(ablk_ref[...], w1_ref[...])
    hid = hid * _sigmoid(hid)
    kv = _nn(hid.astype(BF16), w2_ref[...])
    lane_c = _iota((NC, LANES), 1)
    kcmp_ref[...] = jnp.where(lane_c < 64, _rope(kv, cosc_ref[...], sinc_ref[...], lane_c), 0.0).astype(BF16)
    vcmp_ref[...] = jnp.where(lane_c < 64, pltpu.roll(kv, 64, 1), 0.0).astype(BF16)

    on = _iota((NSEL, 2 * NC), 1) & (NC - 1)
    oj = _iota((NSEL, 2 * NC), 0)
    ovt = jnp.clip(jnp.minimum(on * NSA_CMP_STRIDE + NSA_CMP_LEN, oj * NSA_SEL_BLOCK + NSA_SEL_BLOCK)
                   - jnp.maximum(on * NSA_CMP_STRIDE, oj * NSA_SEL_BLOCK), 0, None).astype(F32) / NSA_CMP_LEN
    ovt = jnp.where(on < NC - 1, ovt, 0.0).astype(BF16)
    jrow = _iota((NSEL, TQ), 0)
    tcol = _iota((NSEL, TQ), 1)

    gr = _iota((2 * LANES, 3 * GROUP_W), 0) & (LANES - 1)
    gc = _iota((2 * LANES, 3 * GROUP_W), 1)
    gate_w = jnp.where(gr == GROUP_HEADS + _div(gc, GROUP_W) * GROUP_HEADS + _div(gc & (GROUP_W - 1), HEAD_DIM),
                       1.0, 0.0).astype(BF16)

    row4 = _iota((H * TQ, TK), 0) & (TQ - 1)
    col4 = _iota((H * TQ, TK), 1)
    lane4 = _iota((H * TQ, LANES), 1)
    lane_q = _iota((TQ, LANES), 1)
    row_q = _iota((TQ, LANES), 0)
    cmp_end = _iota((H * TQ, NC), 1) * NSA_CMP_STRIDE + (NSA_CMP_LEN - 1)
    rowc = _iota((H * TQ, NC), 0) & (TQ - 1)

    def q_body(i, _):
        rq = _rows(i, TQ)
        t0 = i * TQ
        qst = jnp.concatenate([qa_ref[h, rq, :] for h in range(H)], axis=0)

        valid = cmp_end <= (t0 + rowc)
        s = jnp.where(valid, _nt(qst, kcmp_ref[...]), NEG_BIG)
        p = jnp.where(valid, jnp.exp2(s - jnp.max(s, axis=-1, keepdims=True)), 0.0)
        den = jnp.sum(p, axis=-1, keepdims=True)
        p = p / jnp.where(den > 0, den, 1.0)
        o_cmp = _nn(p.astype(BF16), vcmp_ref[...])
        psum = p[0:TQ]
        for h in range(1, H):
            psum = psum + p[h * TQ:(h + 1) * TQ]
        hi, lo = _split2(psum)
        imp = _nt(ovt, jnp.concatenate([hi, lo], axis=1))

        qblk = _div(t0 + tcol, NSA_SEL_BLOCK)
        forced = (jrow == 0) | (jrow == qblk) | (jrow == qblk - 1)
        score = jnp.where(forced, NSA_FORCE, imp)
        score_ref[...] = jnp.where(jrow <= qblk, score, -NSA_FORCE)
        score = score_ref[...]
        rank = jnp.zeros((NSEL, TQ), F32)
        for ii in range(NSEL):
            ci = score_ref[ii:ii + 1, :]
            rank = rank + jnp.where((ci > score) | ((ci == score) & (ii < jrow)), 1.0, 0.0)
        sel = (rank < TOPN) & (jrow <= qblk)
        bias = jnp.where(sel, 0.0, SEL_OFF)
        aug = jnp.concatenate([jnp.zeros((HEAD_DIM, TQ), F32), bias,
                               jnp.zeros((LANES - HEAD_DIM - NSEL, TQ), F32)], axis=0).T.astype(BF16)
        qsel = jnp.where(lane4 < 64, qst, jnp.concatenate([aug] * H, axis=0))

        def att_step(j, _, slot, q_in, k_ref, v_ref, mode):
            rk = _rows(j, TK)
            s = _nt(q_in, k_ref[rk, :])
            if mode is not None:
                kpos = j * TK + col4
                qpos = t0 + row4
                msk = (kpos <= qpos) if mode == "causal" else (kpos > qpos - NSA_WINDOW)
                s = jnp.where(msk, s, NEG_BIG)
            m = m_ref[slot]
            m_new = jnp.maximum(m, jnp.max(s, axis=-1, keepdims=True))
            p = jnp.exp2(jnp.concatenate([s[:, c0:c0 + LANES] - m_new for c0 in range(0, TK, LANES)], axis=1))
            if mode is not None:
                p = jnp.where(msk, p, 0.0)
            acc_ref[slot] = jnp.exp2(m - m_new) * acc_ref[slot] + _nn(p.astype(BF16), v_ref[rk, :])
            m_ref[slot] = m_new
            return 0

        for slot in range(2):
            m_ref[slot] = jnp.full((H * TQ, LANES), NEG_BIG, F32)
            acc_ref[slot] = jnp.zeros((H * TQ, LANES), F32)
        sel_step = functools.partial(att_step, slot=0, q_in=qsel, k_ref=ksel_ref, v_ref=vsel_ref)
        def sel_group(jj, _):
            for u in range(SEL_U):
                sel_step(SEL_U * jj + u, 0, mode=None)
            return 0

        lax.fori_loop(0, _div(i, SEL_U), sel_group, 0)
        for rem in range(SEL_U):
            @pl.when((i & (SEL_U - 1)) == rem)
            def _(rem=rem):
                for u in range(rem, 0, -1):
                    sel_step(i - u, 0, mode=None)
                sel_step(i, 0, mode="causal")

        win_step = functools.partial(att_step, slot=1, q_in=qst, k_ref=kwin_ref, v_ref=vwin_ref)
        for nw in range(WT, -1, -1):
            @pl.when(jnp.minimum(i, WT) == nw)
            def _(nw=nw):
                for w in range(nw, 0, -1):
                    win_step(i - w, 0, mode="window" if w == WT else None)
                win_step(i, 0, mode="causal")

        hi, lo = _split2(_sigmoid(sm_ref[rq, :]))
        gmap = _nn(jnp.concatenate([hi, lo], axis=1), gate_w)
        low = lane_q < 64
        for j in range(H // 2):
            re = slice(2 * j * TQ, (2 * j + 1) * TQ)
            ro = slice((2 * j + 1) * TQ, (2 * j + 2) * TQ)
            out = gmap[:, j * LANES:(j + 1) * LANES] * jnp.where(low, o_cmp[re], pltpu.roll(o_cmp[ro], 64, 1))
            for slot in range(2):
                ae, ao = acc_ref[slot, re, :], acc_ref[slot, ro, :]
                num = jnp.where(low, ae, pltpu.roll(ao, 64, 1))
                den = jnp.where(low, pltpu.roll(ae, 64, 1), ao)
                c0 = (slot + 1) * GROUP_W + j * LANES
                out = out + gmap[:, c0:c0 + LANES] * (num / den)
            o_ref[rq, j * LANES:(j + 1) * LANES] = out.astype(BF16)
        return 0

    lax.fori_loop(0, T // TQ, q_body, 0)


def _rope_tables(pos):
    half = ROT_DIM // 2
    inv_freq = ROPE_THETA ** (-(jnp.arange(half, dtype=F32) * 2.0 / ROT_DIM))
    ang = pos.astype(F32)[:, None] * inv_freq[None, :]
    cos, sin = jnp.cos(ang), jnp.sin(ang)
    n = pos.shape[0]
    c64 = jnp.concatenate([cos, cos, jnp.ones((n, HEAD_DIM - ROT_DIM), F32)], axis=1)
    s64 = jnp.concatenate([-sin, sin, jnp.zeros((n, HEAD_DIM - ROT_DIM), F32)], axis=1)
    return c64, s64


def _nsa(g_ns, g_sm, pe, w1, w2, B, T):
    TQ = TK = 256
    NC = T // NSA_CMP_STRIDE
    c64, s64 = _rope_tables(jnp.arange(T))
    cosq, sinq = jnp.tile(c64, (1, 2)), jnp.tile(s64, (1, 2))
    cosk = jnp.concatenate([c64, jnp.ones((T, HEAD_DIM), F32)], axis=1)
    sink = jnp.concatenate([s64, jnp.zeros((T, HEAD_DIM), F32)], axis=1)
    cc, sc = _rope_tables(jnp.arange(NC) * NSA_CMP_STRIDE + NSA_CMP_LEN - 1)
    cosc = jnp.concatenate([cc, jnp.ones((NC, HEAD_DIM), F32)], axis=1)
    sinc = jnp.concatenate([sc, jnp.zeros((NC, HEAD_DIM), F32)], axis=1)
    kern = functools.partial(_nsa_kernel, T=T, TQ=TQ, TK=TK)
    full = lambda shape: pl.BlockSpec(shape, lambda b: (0,) * len(shape))
    batch = lambda wd: pl.BlockSpec((None, T, wd), lambda b: (b, 0, 0))
    ns_w = GROUP_W + 3 * LANES
    return pl.pallas_call(
        kern,
        grid=(B,),
        in_specs=[batch(ns_w), batch(LANES),
                  full((T, LANES)), full((T, LANES)), full((T, LANES)), full((T, LANES)),
                  full((NC, LANES)), full((NC, LANES)),
                  full((NSA_CMP_LEN, LANES)), full((NSA_CMP_LEN * LANES, LANES)), full((LANES, LANES))],
        out_specs=batch(GROUP_W),
        out_shape=jax.ShapeDtypeStruct((B, T, GROUP_W), BF16),
        scratch_shapes=[pltpu.VMEM((T + 2 * NSA_CMP_STRIDE, LANES), F32),
                        pltpu.VMEM((NC, NSA_CMP_LEN * LANES), BF16),
                        pltpu.VMEM((GROUP_HEADS, T, LANES), BF16),
                        pltpu.VMEM((NC, LANES), BF16), pltpu.VMEM((NC, LANES), BF16)]
        + [pltpu.VMEM((T, LANES), BF16) for _ in range(4)]
        + [pltpu.VMEM((2, GROUP_HEADS * TQ, LANES), F32) for _ in range(2)]
        + [pltpu.VMEM((T // NSA_SEL_BLOCK, TQ), F32)],
        compiler_params=pltpu.CompilerParams(dimension_semantics=("arbitrary",),
                                             vmem_limit_bytes=VMEM_LIMIT),
        name="nsa",
    )(g_ns, g_sm, cosq, sinq, cosk, sink, cosc, sinc, pe, w1, w2)


def _out_kernel(y_hg, y_fx, y_sb, y_ns, gate_ref, x_ref, gn_ref, w_ref, fg_ref, o_ref, *, final):
    bd = jnp.where(_div(_iota((2 * LANES, LANES), 0) & (LANES - 1), HEAD_DIM)
                   == _div(_iota((2 * LANES, LANES), 1), HEAD_DIM), 1.0, 0.0).astype(BF16)
    parts = []
    for ref in (y_hg, y_fx, y_sb, y_ns):
        y = ref[...].astype(F32)
        for j in range(GROUP_W // LANES):
            yy = y[:, j * LANES:(j + 1) * LANES]
            hi, lo = _split2(yy * yy)
            ms = _nn(jnp.concatenate([hi, lo], axis=1), bd) * (1.0 / HEAD_DIM)
            parts.append(yy * lax.rsqrt(ms + EPS))
    gt = gate_ref[...].astype(F32)
    z = jnp.concatenate(parts, axis=1) * gn_ref[...] * (gt * _sigmoid(gt))
    out = x_ref[...] + _nn(z.astype(BF16), w_ref[...])
    if final:
        out = out * lax.rsqrt(jnp.mean(out * out, axis=-1, keepdims=True) + EPS) * fg_ref[...]
    o_ref[...] = out


def _out_project(ys, gate, x2, gn, w_out, fg, final, tm):
    n = x2.shape[0]
    kern = functools.partial(_out_kernel, final=final)
    row = lambda wd: pl.BlockSpec((tm, wd), lambda i: (i, 0))
    return pl.pallas_call(
        kern,
        grid=(n // tm,),
        in_specs=[row(GROUP_W)] * 4 + [row(D_MODEL), row(D_MODEL),
                                       pl.BlockSpec((1, D_MODEL), lambda i: (0, 0)),
                                       pl.BlockSpec((D_MODEL, D_MODEL), lambda i: (0, 0)),
                                       pl.BlockSpec((1, D_MODEL), lambda i: (0, 0))],
        out_specs=row(D_MODEL),
        out_shape=jax.ShapeDtypeStruct((n, D_MODEL), F32),
        compiler_params=pltpu.CompilerParams(dimension_semantics=("arbitrary",),
                                             vmem_limit_bytes=VMEM_LIMIT),
        name="outproj",
    )(*ys, gate, x2, gn.reshape(1, D_MODEL), w_out, fg.reshape(1, D_MODEL))


def _pack_in_weights(w):
    o = IN_OFFS
    sm = jnp.concatenate([w[:, o[6]:o[7]], w[:, o[17]:o[18]],
                          jnp.zeros((D_MODEL, LANES - GROUP_HEADS - 3 * GROUP_HEADS), w.dtype)], axis=1)
    groups = [w[:, o[0]:o[3]], w[:, o[3]:o[6]], w[:, o[7]:o[10]], w[:, o[10]:o[17]], w[:, o[18]:o[19]], sm]
    return [g.astype(BF16) for g in groups]


def _pack_cmp_weights(pe_k, w1_k, w2_k, pe_v, w1_v, w2_v):
    L, d = NSA_CMP_LEN, HEAD_DIM
    z = jnp.zeros((L, d, d), F32)
    w1 = jnp.concatenate([jnp.concatenate([w1_k.reshape(L, d, d), z], axis=2),
                          jnp.concatenate([z, w1_v.reshape(L, d, d)], axis=2)], axis=1).reshape(L * 2 * d, 2 * d)
    z2 = jnp.zeros((d, d), F32)
    w2 = jnp.concatenate([jnp.concatenate([w2_k, z2], axis=1), jnp.concatenate([z2, w2_v], axis=1)], axis=0)
    return jnp.concatenate([pe_k, pe_v], axis=1), w1.astype(BF16), w2.astype(BF16)


def kernel(x, norm_g, w_in, hgrn_lb_logits, fox_fb, nsa_cmp_pe_k, nsa_cmp_w1_k, nsa_cmp_w2_k,
           nsa_cmp_pe_v, nsa_cmp_w1_v, nsa_cmp_w2_v, out_norm_g, w_out, final_norm_g):
    B, T, D = x.shape
    depth = w_in.shape[0]
    x2 = x.reshape(B * T, D)
    for l in range(depth):
        g_hg, g_fx, g_sb, g_ns, g_gate, g_sm = _project(x2, norm_g[l], _pack_in_weights(w_in[l]), 512)
        g_sm3 = g_sm.reshape(B, T, LANES)
        fb = jnp.concatenate([fox_fb[l], jnp.zeros((LANES - GROUP_HEADS,), F32)]).reshape(1, LANES)
        pe, w1, w2 = _pack_cmp_weights(nsa_cmp_pe_k[l], nsa_cmp_w1_k[l], nsa_cmp_w2_k[l],
                                       nsa_cmp_pe_v[l], nsa_cmp_w1_v[l], nsa_cmp_w2_v[l])
        y_hg = _hgrn2(g_hg.reshape(B, T, -1), hgrn_lb_logits, l, B, T)
        y_fx, y_sb = _fox_stick_breaking(g_fx.reshape(B, T, -1), g_sm3, fb, g_sb.reshape(B, T, -1), B, T)
        y_ns = _nsa(g_ns.reshape(B, T, -1), g_sm3, pe, w1, w2, B, T)
        ys = [y.reshape(B * T, GROUP_W) for y in (y_hg, y_fx, y_sb, y_ns)]
        x2 = _out_project(ys, g_gate, x2, out_norm_g[l], w_out[l].astype(BF16), final_norm_g,
                          l == depth - 1, 512)
    return x2.reshape(B, T, D)
```

```python
import functools

import numpy as np
import jax
import jax.numpy as jnp
from jax import lax
from jax.experimental import pallas as pl
from jax.experimental.pallas import tpu as pltpu

F32 = jnp.float32
BF16 = jnp.bfloat16

D_MODEL = 1024
HEAD_DIM = 64
GROUP_W = 256
GROUP_HEADS = 4
ATTN_SCALE = HEAD_DIM ** -0.5
LOG2E = 1.4426950408889634
SCALE_LOG2 = ATTN_SCALE * LOG2E
ROPE_THETA = 500000.0
ROT_DIM = HEAD_DIM // 4
HG_CHUNK = 64
NSA_CMP_LEN = 32
NSA_CMP_STRIDE = 16
NSA_SEL_BLOCK = 64
NSA_TOP_N = 16
NSA_WINDOW = 512
NSA_FORCE = 1.0e4
NEG_BIG = -1.0e30
SEL_OFF = -(2.0 ** 100)
EPS = 1e-6

LANES = 128
VMEM_LIMIT = 56 * 1024 * 1024

IN_WIDTHS = (GROUP_W,) * 3 + (GROUP_W,) * 3 + (GROUP_HEADS,) + (GROUP_W,) * 3 + (GROUP_W,) \
    + (HEAD_DIM,) * 6 + (3 * GROUP_HEADS,) + (D_MODEL,)
IN_OFFS = np.concatenate([[0], np.cumsum(IN_WIDTHS)]).tolist()
GATE_GROUP = 4


def _nn(a, b):
    return jnp.dot(a, b, preferred_element_type=F32)


def _nt(a, b):
    return lax.dot_general(a, b, (((1,), (1,)), ((), ())), preferred_element_type=F32)


def _tn(a, b):
    return lax.dot_general(a, b, (((0,), (0,)), ((), ())), preferred_element_type=F32)


def _iota(shape, dim):
    return lax.broadcasted_iota(jnp.int32, shape, dim)


def _split2(x):
    hi = x.astype(BF16)
    lo = (x - hi.astype(F32)).astype(BF16)
    return hi, lo


def _split3(x):
    p1 = x.astype(BF16).astype(F32)
    r = x - p1
    p2 = r.astype(BF16).astype(F32)
    p3 = (r - p2).astype(BF16).astype(F32)
    return p1, p2, p3


def _sigmoid(x):
    return 1.0 / (1.0 + jnp.exp(-x))


def _log_sigmoid(x):
    return jnp.minimum(x, 0.0) - jnp.log(1.0 + jnp.exp(-jnp.abs(x)))


def _div(x, n):
    assert n & (n - 1) == 0
    return x >> (n.bit_length() - 1)


def _rows(i, n):
    return pl.ds(pl.multiple_of(i * n, n), n)


def _head_to_slot(x, h):
    return x if h % 2 == 0 else pltpu.roll(x, HEAD_DIM, 1)


def _proj_kernel(x_ref, g_ref, w_hg, w_fx, w_sb, w_ns, w_gate, w_sm,
                 o_hg, o_fx, o_sb, o_ns, o_gate, o_sm):
    x = x_ref[...]
    h = x * lax.rsqrt(jnp.mean(x * x, axis=-1, keepdims=True) + EPS) * g_ref[...]
    hb = h.astype(BF16)
    for w, o in ((w_hg, o_hg), (w_fx, o_fx), (w_sb, o_sb), (w_ns, o_ns), (w_gate, o_gate), (w_sm, o_sm)):
        o[...] = _nn(hb, w[...]).astype(o.dtype)


def _project(x2, norm_g, ws, tm):
    n = x2.shape[0]
    widths = [w.shape[1] for w in ws]
    return pl.pallas_call(
        _proj_kernel,
        grid=(n // tm,),
        in_specs=[pl.BlockSpec((tm, D_MODEL), lambda i: (i, 0)),
                  pl.BlockSpec((1, D_MODEL), lambda i: (0, 0))]
        + [pl.BlockSpec((D_MODEL, wd), lambda i: (0, 0)) for wd in widths],
        out_specs=[pl.BlockSpec((tm, wd), lambda i: (i, 0)) for wd in widths],
        out_shape=[jax.ShapeDtypeStruct((n, wd), BF16 if gi == GATE_GROUP else F32) for gi, wd in enumerate(widths)],
        compiler_params=pltpu.CompilerParams(dimension_semantics=("arbitrary",),
                                             vmem_limit_bytes=VMEM_LIMIT),
        name="proj",
    )(x2, norm_g.reshape(1, D_MODEL), *ws)


def _fox_program(g_ref, sm_ref, fb_ref, o_ref, cum_ref, qa_ref, ka_ref, va_ref, m_ref, acc_ref, *, T, TQ, TK):
    H = GROUP_HEADS
    RB = 256
    NB = T // LANES
    tri = jnp.where(_iota((LANES, LANES), 0) >= _iota((LANES, LANES), 1), 1.0, 0.0).astype(BF16)
    fb = fb_ref[...]

    local = []
    for i in range(NB):
        p1, p2, p3 = _split3(_log_sigmoid(sm_ref[i * LANES:(i + 1) * LANES, :] + fb))
        local.append(_nn(tri, p1.astype(BF16)) + _nn(tri, p2.astype(BF16)) + _nn(tri, p3.astype(BF16)))
    carry = jnp.zeros((1, LANES), F32)
    for i in range(NB):
        c = local[i] + carry
        cum_ref[i * LANES:(i + 1) * LANES, :] = c
        carry = c[LANES - 1:LANES, :]

    lane = _iota((RB, LANES), 1)
    upper = lane >= HEAD_DIM

    def build_body(i, _):
        r = _rows(i, RB)
        cum = cum_ref[r, :]
        for h in range(H):
            j = h // 2
            own = upper if h % 2 else ~upper
            a0 = 0 if h % 2 else HEAD_DIM
            q = g_ref[r, j * LANES:(j + 1) * LANES] * SCALE_LOG2
            k = g_ref[r, GROUP_W + j * LANES:GROUP_W + (j + 1) * LANES]
            v = g_ref[r, 2 * GROUP_W + j * LANES:2 * GROUP_W + (j + 1) * LANES]
            qa = jnp.where(own, q, 0.0)
            ka = jnp.where(own, k, 0.0)
            for p, c in enumerate(_split3(jnp.broadcast_to(cum[:, h:h + 1], (RB, LANES)) * LOG2E)):
                qa = jnp.where(lane == a0 + p, c, qa)
                qa = jnp.where(lane == a0 + 3 + p, 1.0, qa)
                ka = jnp.where(lane == a0 + p, 1.0, ka)
                ka = jnp.where(lane == a0 + 3 + p, -c, ka)
            qa_ref[h, r, :] = qa.astype(BF16)
            ka_ref[h, r, :] = ka.astype(BF16)
            va_ref[h, r, :] = jnp.where(own, v, 1.0).astype(BF16)
        return 0

    lax.fori_loop(0, T // RB, build_body, 0)

    NSUB = TQ // TK
    upper_q =_iota((TQ, LANES), 1) >= HEAD_DIM

    def start(i):
        for h in range(H):
            m_ref[h] = jnp.full((TQ, LANES), NEG_BIG, F32)
            acc_ref[h] = jnp.zeros((TQ, LANES), F32)
        return [qa_ref[h, _rows(i, TQ), :] for h in range(H)]

    def k_step(qas, i, j, masked, r0=0):
        rk = _rows(j, TK)
        rs = slice(r0, TQ)
        for h in range(H):
            s = _nt(qas[h][rs], ka_ref[h, rk, :])
            if masked:
                off = _iota((TQ - r0, TK), 1) - _iota((TQ - r0, TK), 0)
                s = jnp.where(off <= i * TQ + r0 - j * TK, s, NEG_BIG)
            m = m_ref[h, rs, :]
            m_new = jnp.maximum(m, jnp.max(s, axis=-1, keepdims=True))
            p = jnp.exp2(jnp.concatenate([s[:, c0:c0 + LANES] - m_new for c0 in range(0, TK, LANES)], axis=1))
            acc_ref[h, rs, :] = jnp.exp2(m - m_new) * acc_ref[h, rs, :] + _nn(p.astype(BF16), va_ref[h, rk, :])
            m_ref[h, rs, :] = m_new

    def diag(qas, i):
        for d in range(NSUB):
            k_step(qas, i, i * NSUB + d, True, r0=d * TK)

    def pair(qas, i, jj):
        for d in range(NSUB):
            k_step(qas, i, jj * NSUB + d, False)

    def finish(i):
        for j in range(H // 2):
            ae, ao = acc_ref[2 * j], acc_ref[2 * j + 1]
            den = pltpu.roll(jnp.where(upper_q, ae, ao), HEAD_DIM, 1)
            o_ref[_rows(i, TQ), j * LANES:(j + 1) * LANES] = (jnp.where(upper_q, ao, ae) / den).astype(BF16)

    return start, diag, pair, finish


def _sb_program(g_ref, o_ref, qa_ref, ka_ref, va_ref, c_ref, acc_ref, *, T, TQ, TK):
    H = GROUP_HEADS
    RB = 256
    SUB = LANES
    NSUB = TQ // TK
    upper = _iota((RB, LANES), 1) >= HEAD_DIM

    def build_body(i, _):
        r = _rows(i, RB)
        for h in range(H):
            j = h // 2
            own = upper if h % 2 else ~upper
            q = g_ref[r, j * LANES:(j + 1) * LANES] * ATTN_SCALE
            k = g_ref[r, GROUP_W + j * LANES:GROUP_W + (j + 1) * LANES]
            v = g_ref[r, 2 * GROUP_W + j * LANES:2 * GROUP_W + (j + 1) * LANES]
            qa_ref[h, r, :] = jnp.where(own, q, 0.0).astype(BF16)
            ka_ref[h, r, :] = jnp.where(own, k, 0.0).astype(BF16)
            va_ref[h, r, :] = jnp.where(own, v, 0.0).astype(BF16)
        return 0

    lax.fori_loop(0, T // RB, build_body, 0)

    wr = _iota((2 * SUB, 2 * SUB), 0) & (SUB - 1)
    wc = _iota((2 * SUB, 2 * SUB), 1)
    suf_w = jnp.where((wc >= SUB) | (wr > wc), 1.0, 0.0).astype(BF16)

    def start(i):
        for h in range(H):
            c_ref[h] = jnp.zeros((TQ, LANES), F32)
        for j in range(H // 2):
            acc_ref[j] = jnp.zeros((TQ, LANES), F32)
        return [qa_ref[h, _rows(i, TQ), :] for h in range(H)]

    def k_step(qas, i, j, masked, r0=0):
        rk = _rows(j, TK)
        rs = slice(r0, TQ)
        for h in range(H):
            z = _nt(qas[h][rs], ka_ref[h, rk, :])
            ls = jnp.minimum(z, 0.0) - jnp.log(1.0 + jnp.exp2(jnp.abs(z) * -LOG2E))
            lom = ls - z
            if masked:
                off = _iota((TQ - r0, TK), 1) - _iota((TQ - r0, TK), 0)
                msk = off < i * TQ + r0 - j * TK
                lom = jnp.where(msk, lom, 0.0)
            c = c_ref[h, rs, :]
            parts = []
            for sb in reversed(range(TK // SUB)):
                hi, lo = _split2(lom[:, sb * SUB:(sb + 1) * SUB])
                cs = _nn(jnp.concatenate([hi, lo], axis=1), suf_w)
                parts.append(ls[:, sb * SUB:(sb + 1) * SUB] + cs[:, :SUB] + c)
                c = c + cs[:, SUB:]
            c_ref[h, rs, :] = c
            a = jnp.exp(jnp.concatenate(parts[::-1], axis=1))
            if masked:
                a = jnp.where(msk, a, 0.0)
            acc_ref[h // 2, rs, :] += _nn(a.astype(BF16), va_ref[h, rk, :])

    def diag(qas, i):
        for d in reversed(range(NSUB)):
            k_step(qas, i, i * NSUB + d, True, r0=d * TK)

    def pair(qas, i, jj):
        for d in range(NSUB):
            k_step(qas, i, (i - jj) * NSUB - 1 - d, False)

    def finish(i):
        for j in range(H // 2):
            o_ref[_rows(i, TQ), j * LANES:(j + 1) * LANES] = acc_ref[j].astype(BF16)

    return start, diag, pair, finish


def _fox_sb_kernel(fx_ref, sm_ref, fb_ref, sb_ref, ofx_ref, osb_ref,
                   cum_ref, fqa, fka, fva, fm, facc, sqa, ska, sva, sc, sacc, *, T, TQ, TK):
    fox = _fox_program(fx_ref, sm_ref, fb_ref, ofx_ref, cum_ref, fqa, fka, fva, fm, facc, T=T, TQ=TQ, TK=TK)
    stb = _sb_program(sb_ref, osb_ref, sqa, ska, sva, sc, sacc, T=T, TQ=TQ, TK=TK)
    progs = (fox, stb)

    def q_body(i, _):
        ctx = [p[0](i) for p in progs]
        for p, qas in zip(progs, ctx):
            p[1](qas, i)

        def k_body(jj, _):
            for p, qas in zip(progs, ctx):
                p[2](qas, i, jj)
            return 0

        lax.fori_loop(0, i, k_body, 0)
        for p in progs:
            p[3](i)
        return 0

    lax.fori_loop(0, T // TQ, q_body, 0)


def _fox_stick_breaking(g_fx, g_sm, fb, g_sb, B, T):
    TQ, TK = 512, 256
    kern = functools.partial(_fox_sb_kernel, T=T, TQ=TQ, TK=TK)
    slot = lambda: pltpu.VMEM((GROUP_HEADS, T, LANES), BF16)
    state = lambda n: pltpu.VMEM((n, TQ, LANES), F32)
    batch = lambda wd: pl.BlockSpec((None, T, wd), lambda b: (b, 0, 0))
    return pl.pallas_call(
        kern,
        grid=(B,),
        in_specs=[batch(3 * GROUP_W), batch(LANES), pl.BlockSpec((1, LANES), lambda b: (0, 0)), batch(3 * GROUP_W)],
        out_specs=[batch(GROUP_W), batch(GROUP_W)],
        out_shape=[jax.ShapeDtypeStruct((B, T, GROUP_W), BF16)] * 2,
        scratch_shapes=[pltpu.VMEM((T, LANES), F32), slot(), slot(), slot(), state(GROUP_HEADS), state(GROUP_HEADS),
                        slot(), slot(), slot(), state(GROUP_HEADS), state(GROUP_HEADS // 2)],
        compiler_params=pltpu.CompilerParams(dimension_semantics=("arbitrary",),
                                             vmem_limit_bytes=VMEM_LIMIT),
        name="fox_stickbreak",
    )(g_fx, g_sm, fb, g_sb)


def _hgrn_decay_matrix(C):
    nl = C.bit_length() - 1
    t = _iota((C, C), 0)
    u = _iota((C, C), 1)
    blocks = [u <= t, u > t]
    q_blocks, k_blocks = [], []
    for l in range(nl):
        m = C >> (l + 1)
        bnd = (t & ~(2 * m - 1)) + (m - 1)
        q_blocks.append((u > bnd) & (u <= t))
        k_blocks.append((u > t) & (u <= bnd))
    d = jnp.concatenate([jnp.where(b, 1.0, 0.0) for b in blocks + q_blocks + k_blocks], axis=0).astype(BF16)
    return jnp.concatenate([d, d, d], axis=1)


def _hgrn_kernel(g_ref, lb_ref, o_ref, st_ref, ex_ref, *, T, C, NCH, layer):
    H = GROUP_HEADS
    nl = C.bit_length() - 1
    lg = lb_ref[...]
    e = jnp.exp(lg - jnp.max(lg, axis=0, keepdims=True))
    sm = e / jnp.sum(e, axis=0, keepdims=True)
    lb = jnp.sum(sm[0:layer + 1, :], axis=0, keepdims=True) - sm[0:1, :]

    dmat = _hgrn_decay_matrix(C)
    lane_head4 = _div(_iota((H * C, GROUP_W), 1), HEAD_DIM)
    row_head4 = _div(_iota((H * C, GROUP_W), 0), C)
    head_sel = lane_head4 == row_head4
    tq_c = _iota((C, GROUP_W), 0)
    t4 = _iota((C, H * C), 0)
    s4 = _iota((C, H * C), 1) & (C - 1)
    bd_mask =_div(_iota((GROUP_W, GROUP_W), 0), HEAD_DIM) == _div(_iota((GROUP_W, GROUP_W), 1), HEAD_DIM)
    st_ref[...] = jnp.zeros((GROUP_W, GROUP_W), F32)

    def decays(r, slot, n):
        g = jnp.log(lb + (1.0 - lb) * _sigmoid(g_ref[r, GROUP_W:2 * GROUP_W]))
        g1, g2, g3 = _split3(g)
        ex_ref[slot, n] = _nn(dmat, jnp.concatenate([g1.astype(BF16), g2.astype(BF16), g3.astype(BF16)], axis=0))

    def intra(r, slot, n):
        ex = ex_ref.at[slot, n]
        qraw = g_ref[r, 0:GROUP_W]
        f = g_ref[r, GROUP_W:2 * GROUP_W]
        v = g_ref[r, 2 * GROUP_W:3 * GROUP_W].astype(BF16)
        q = qraw * _sigmoid(qraw)
        k = (1.0 - lb) * (1.0 - _sigmoid(f))
        b = ex[0:C]

        a = jnp.zeros((C, H * C), F32)
        for l in range(nl + 1):
            if l < nl:
                m = C >> (l + 1)
                dq = ex[(2 + l) * C:(3 + l) * C]
                dk = ex[(2 + nl + l) * C:(3 + nl + l) * C]
                qs = q * jnp.where((tq_c & m) != 0, jnp.exp(dq), 0.0)
                ks = k * jnp.where((tq_c & m) == 0, jnp.exp(dk), 0.0)
                pair = (t4 & ~(2 * m - 1)) == (s4 & ~(2 * m - 1))
            else:
                qs, ks = q, k
                pair = t4 == s4
            ksb = ks.astype(BF16)
            kst = jnp.where(head_sel, jnp.concatenate([ksb] * H, axis=0), jnp.zeros((), BF16))
            a = a + jnp.where(pair, _nt(qs.astype(BF16), kst), 0.0)
        vbd = jnp.where(head_sel, jnp.concatenate([v] * H, axis=0), jnp.zeros((), BF16))
        o_intra = _nn(a.astype(BF16), vbd)
        qb =(q * jnp.exp(b)).astype(BF16)
        kd = (k * jnp.exp(ex[C:2 * C])).astype(BF16)
        upd = jnp.where(bd_mask, _tn(v, kd), 0.0)
        return o_intra, qb, upd, jnp.exp(b[C - 1:C, :])

    NIT = T // (C * NCH)
    for n in range(NCH):
        decays(_rows(n, C), 0, n)

    def body(ci, _):
        slot = ci & 1
        nxt = jnp.minimum(ci + 1, NIT - 1)
        for n in range(NCH):
            decays(_rows(nxt * NCH + n, C), 1 - slot, n)
        rs = [_rows(ci * NCH + n, C) for n in range(NCH)]
        parts = [intra(r, slot, n) for n, r in enumerate(rs)]
        st = st_ref[...]
        for r, (o_intra, qb, upd, decay) in zip(rs, parts):
            o_ref[r, :] = (o_intra + _nt(qb, st.astype(BF16))).astype(BF16)
            st = st * decay + upd
        st_ref[...] = st
        return 0

    lax.fori_loop(0, NIT, body, 0)


def _hgrn2(g_hg, lb_logits, layer, B, T):
    depth = lb_logits.shape[0]
    NCH = 8
    kern = functools.partial(_hgrn_kernel, T=T, C=HG_CHUNK, NCH=NCH, layer=layer)
    return pl.pallas_call(
        kern,
        grid=(B,),
        in_specs=[pl.BlockSpec((None, T, 3 * GROUP_W), lambda b: (b, 0, 0)),
                  pl.BlockSpec((depth, GROUP_W), lambda b: (0, 0))],
        out_specs=pl.BlockSpec((None, T, GROUP_W), lambda b: (b, 0, 0)),
        out_shape=jax.ShapeDtypeStruct((B, T, GROUP_W), BF16),
        scratch_shapes=[pltpu.VMEM((GROUP_W, GROUP_W), F32),
                        pltpu.VMEM((2, NCH, 2 * HG_CHUNK.bit_length() * HG_CHUNK, GROUP_W), F32)],
        compiler_params=pltpu.CompilerParams(dimension_semantics=("arbitrary",),
                                             vmem_limit_bytes=VMEM_LIMIT),
        name="hgrn2",
    )(g_hg, lb_logits)


def _rope(x, cos, sin, lane):
    swapped = jnp.where((lane & (HEAD_DIM - 1)) < ROT_DIM // 2,
                        pltpu.roll(x, LANES - ROT_DIM // 2, 1), pltpu.roll(x, ROT_DIM // 2, 1))
    return x * cos + swapped * sin


def _nsa_kernel(g_ref, sm_ref, cosq_ref, sinq_ref, cosk_ref, sink_ref, cosc_ref, sinc_ref,
                pe_ref, w1_ref, w2_ref, o_ref,
                cv_ref, ablk_ref, qa_ref, kcmp_ref, vcmp_ref, ksel_ref, vsel_ref, kwin_ref, vwin_ref,
                m_ref, acc_ref, score_ref, *, T, TQ, TK):
    H = GROUP_HEADS
    RB = 256
    NC = T // NSA_CMP_STRIDE
    NSEL = T // NSA_SEL_BLOCK
    TOPN = min(NSA_TOP_N, NSEL)
    WT = NSA_WINDOW // TK
    SEL_U = 8
    lane = _iota((RB, LANES), 1)
    grow = _iota((RB, LANES), 0)

    def build_body(i, _):
        r = _rows(i, RB)
        cosq, sinq = cosq_ref[r, :], sinq_ref[r, :]
        cosk, sink = cosk_ref[r, :], sink_ref[r, :]
        for j in range(2):
            xr = _rope(g_ref[r, j * LANES:(j + 1) * LANES], cosq, sinq, lane) * SCALE_LOG2
            for h in (2 * j, 2 * j + 1):
                qa_ref[h, r, :] = jnp.where(lane < HEAD_DIM, _head_to_slot(xr, h), 0.0).astype(BF16)
        cv_ref[r, :] = g_ref[r, GROUP_W:GROUP_W + LANES]
        ksvs = g_ref[r, GROUP_W + LANES:GROUP_W + 2 * LANES]
        kwvw = g_ref[r, GROUP_W + 2 * LANES:GROUP_W + 3 * LANES]
        blk = _div(i * RB + grow, NSA_SEL_BLOCK)
        onehot = jnp.where((lane >= HEAD_DIM) & (lane - HEAD_DIM == blk), 1.0, 0.0)
        ksel_ref[r, :] = jnp.where(lane < HEAD_DIM, _rope(ksvs, cosk, sink, lane), onehot).astype(BF16)
        vsel_ref[r, :] = jnp.where(lane < HEAD_DIM, pltpu.roll(ksvs, HEAD_DIM, 1), 1.0).astype(BF16)
        kwin_ref[r, :] = jnp.where(lane < HEAD_DIM, _rope(kwvw, cosk, sink, lane), 0.0).astype(BF16)
        vwin_ref[r, :] = jnp.where(lane < HEAD_DIM, pltpu.roll(kwvw, HEAD_DIM, 1), 1.0).astype(BF16)
        return 0

    lax.fori_loop(0, T // RB, build_body, 0)
    cv_ref[T:T + 2 * NSA_CMP_STRIDE, :] = jnp.zeros((2 * NSA_CMP_STRIDE, LANES), F32)

    for l in range(NSA_CMP_LEN):
        blk_l = cv_ref[pl.ds(l, NC, stride=NSA_CMP_STRIDE), :] + pe_ref[l:l + 1, :]
        ablk_ref[:, l * LANES:(l + 1) * LANES] = blk_l.astype(BF16)
    hid = _nn(ablk_ref[...], w1_ref[...])
    hid = hid * _sigmoid(hid)
    kv = _nn(hid.astype(BF16), w2_ref[...])
    lane_c = _iota((NC, LANES), 1)
    kcmp_ref[...] = jnp.where(lane_c < HEAD_DIM, _rope(kv, cosc_ref[...], sinc_ref[...], lane_c), 0.0).astype(BF16)
    vcmp_ref[...] = jnp.where(lane_c < HEAD_DIM, pltpu.roll(kv, HEAD_DIM, 1), 0.0).astype(BF16)

    on = _iota((NSEL, 2 * NC), 1) & (NC - 1)
    oj = _iota((NSEL, 2 * NC), 0)
    ovt = jnp.clip(jnp.minimum(on * NSA_CMP_STRIDE + NSA_CMP_LEN, oj * NSA_SEL_BLOCK + NSA_SEL_BLOCK)
                   - jnp.maximum(on * NSA_CMP_STRIDE, oj * NSA_SEL_BLOCK), 0, None).astype(F32) / NSA_CMP_LEN
    ovt = jnp.where(on < NC - 1, ovt, 0.0).astype(BF16)
    jrow = _iota((NSEL, TQ), 0)
    tcol = _iota((NSEL, TQ), 1)

    gr = _iota((2 * LANES, 3 * GROUP_W), 0) & (LANES - 1)
    gc = _iota((2 * LANES, 3 * GROUP_W), 1)
    gate_w = jnp.where(gr == GROUP_HEADS + _div(gc, GROUP_W) * GROUP_HEADS + _div(gc & (GROUP_W - 1), HEAD_DIM),
                       1.0, 0.0).astype(BF16)

    row4 = _iota((H * TQ, TK), 0) & (TQ - 1)
    col4 = _iota((H * TQ, TK), 1)
    lane4 = _iota((H * TQ, LANES), 1)
    lane_q = _iota((TQ, LANES), 1)
    row_q = _iota((TQ, LANES), 0)
    cmp_end = _iota((H * TQ, NC), 1) * NSA_CMP_STRIDE + (NSA_CMP_LEN - 1)
    rowc = _iota((H * TQ, NC), 0) & (TQ - 1)

    def q_body(i, _):
        rq = _rows(i, TQ)
        t0 = i * TQ
        qst = jnp.concatenate([qa_ref[h, rq, :] for h in range(H)], axis=0)

        valid = cmp_end <= (t0 + rowc)
        s = jnp.where(valid, _nt(qst, kcmp_ref[...]), NEG_BIG)
        p = jnp.where(valid, jnp.exp2(s - jnp.max(s, axis=-1, keepdims=True)), 0.0)
        den = jnp.sum(p, axis=-1, keepdims=True)
        p = p / jnp.where(den > 0, den, 1.0)
        o_cmp = _nn(p.astype(BF16), vcmp_ref[...])
        psum = p[0:TQ]
        for h in range(1, H):
            psum = psum + p[h * TQ:(h + 1) * TQ]
        hi, lo = _split2(psum)
        imp = _nt(ovt, jnp.concatenate([hi, lo], axis=1))

        qblk = _div(t0 + tcol, NSA_SEL_BLOCK)
        forced = (jrow == 0) | (jrow == qblk) | (jrow == qblk - 1)
        score = jnp.where(forced, NSA_FORCE, imp)
        score_ref[...] = jnp.where(jrow <= qblk, score, -NSA_FORCE)
        score = score_ref[...]
        rank = jnp.zeros((NSEL, TQ), F32)
        for ii in range(NSEL):
            ci = score_ref[ii:ii + 1, :]
            rank = rank + jnp.where((ci > score) | ((ci == score) & (ii < jrow)), 1.0, 0.0)
        sel = (rank < TOPN) & (jrow <= qblk)
        bias = jnp.where(sel, 0.0, SEL_OFF)
        aug = jnp.concatenate([jnp.zeros((HEAD_DIM, TQ), F32), bias,
                               jnp.zeros((LANES - HEAD_DIM - NSEL, TQ), F32)], axis=0).T.astype(BF16)
        qsel = jnp.where(lane4 < HEAD_DIM, qst, jnp.concatenate([aug] * H, axis=0))

        def att_step(j, _, slot, q_in, k_ref, v_ref, mode):
            rk = _rows(j, TK)
            s = _nt(q_in, k_ref[rk, :])
            if mode is not None:
                kpos = j * TK + col4
                qpos = t0 + row4
                msk = (kpos <= qpos) if mode == "causal" else (kpos > qpos - NSA_WINDOW)
                s = jnp.where(msk, s, NEG_BIG)
            m = m_ref[slot]
            m_new = jnp.maximum(m, jnp.max(s, axis=-1, keepdims=True))
            p = jnp.exp2(jnp.concatenate([s[:, c0:c0 + LANES] - m_new for c0 in range(0, TK, LANES)], axis=1))
            if mode is not None:
                p = jnp.where(msk, p, 0.0)
            acc_ref[slot] = jnp.exp2(m - m_new) * acc_ref[slot] + _nn(p.astype(BF16), v_ref[rk, :])
            m_ref[slot] = m_new
            return 0

        for slot in range(2):
            m_ref[slot] = jnp.full((H * TQ, LANES), NEG_BIG, F32)
            acc_ref[slot] = jnp.zeros((H * TQ, LANES), F32)
        sel_step = functools.partial(att_step, slot=0, q_in=qsel, k_ref=ksel_ref, v_ref=vsel_ref)
        def sel_group(jj, _):
            for u in range(SEL_U):
                sel_step(SEL_U * jj + u, 0, mode=None)
            return 0

        lax.fori_loop(0, _div(i, SEL_U), sel_group, 0)
        for rem in range(SEL_U):
            @pl.when((i & (SEL_U - 1)) == rem)
            def _(rem=rem):
                for u in range(rem, 0, -1):
                    sel_step(i - u, 0, mode=None)
                sel_step(i, 0, mode="causal")

        win_step = functools.partial(att_step, slot=1, q_in=qst, k_ref=kwin_ref, v_ref=vwin_ref)
        for nw in range(WT, -1, -1):
            @pl.when(jnp.minimum(i, WT) == nw)
            def _(nw=nw):
                for w in range(nw, 0, -1):
                    win_step(i - w, 0, mode="window" if w == WT else None)
                win_step(i, 0, mode="causal")

        hi, lo = _split2(_sigmoid(sm_ref[rq, :]))
        gmap = _nn(jnp.concatenate([hi, lo], axis=1), gate_w)
        low = lane_q < HEAD_DIM
        for j in range(H // 2):
            re = slice(2 * j * TQ, (2 * j + 1) * TQ)
            ro = slice((2 * j + 1) * TQ, (2 * j + 2) * TQ)
            out = gmap[:, j * LANES:(j + 1) * LANES] * jnp.where(low, o_cmp[re], pltpu.roll(o_cmp[ro], HEAD_DIM, 1))
            for slot in range(2):
                ae, ao = acc_ref[slot, re, :], acc_ref[slot, ro, :]
                num = jnp.where(low, ae, pltpu.roll(ao, HEAD_DIM, 1))
                den = jnp.where(low, pltpu.roll(ae, HEAD_DIM, 1), ao)
                c0 = (slot + 1) * GROUP_W + j * LANES
                out = out + gmap[:, c0:c0 + LANES] * (num / den)
            o_ref[rq, j * LANES:(j + 1) * LANES] = out.astype(BF16)
        return 0

    lax.fori_loop(0, T // TQ, q_body, 0)


def _rope_tables(pos):
    half = ROT_DIM // 2
    inv_freq = ROPE_THETA ** (-(jnp.arange(half, dtype=F32) * 2.0 / ROT_DIM))
    ang = pos.astype(F32)[:, None] * inv_freq[None, :]
    cos, sin = jnp.cos(ang), jnp.sin(ang)
    n = pos.shape[0]
    c64 = jnp.concatenate([cos, cos, jnp.ones((n, HEAD_DIM - ROT_DIM), F32)], axis=1)
    s64 = jnp.concatenate([-sin, sin, jnp.zeros((n, HEAD_DIM - ROT_DIM), F32)], axis=1)
    return c64, s64


def _nsa(g_ns, g_sm, pe, w1, w2, B, T):
    TQ = TK = 256
    NC = T // NSA_CMP_STRIDE
    c64, s64 = _rope_tables(jnp.arange(T))
    cosq, sinq = jnp.tile(c64, (1, 2)), jnp.tile(s64, (1, 2))
    cosk = jnp.concatenate([c64, jnp.ones((T, HEAD_DIM), F32)], axis=1)
    sink = jnp.concatenate([s64, jnp.zeros((T, HEAD_DIM), F32)], axis=1)
    cc, sc = _rope_tables(jnp.arange(NC) * NSA_CMP_STRIDE + NSA_CMP_LEN - 1)
    cosc = jnp.concatenate([cc, jnp.ones((NC, HEAD_DIM), F32)], axis=1)
    sinc = jnp.concatenate([sc, jnp.zeros((NC, HEAD_DIM), F32)], axis=1)
    kern = functools.partial(_nsa_kernel, T=T, TQ=TQ, TK=TK)
    full = lambda shape: pl.BlockSpec(shape, lambda b: (0,) * len(shape))
    batch = lambda wd: pl.BlockSpec((None, T, wd), lambda b: (b, 0, 0))
    ns_w = GROUP_W + 3 * LANES
    return pl.pallas_call(
        kern,
        grid=(B,),
        in_specs=[batch(ns_w), batch(LANES),
                  full((T, LANES)), full((T, LANES)), full((T, LANES)), full((T, LANES)),
                  full((NC, LANES)), full((NC, LANES)),
                  full((NSA_CMP_LEN, LANES)), full((NSA_CMP_LEN * LANES, LANES)), full((LANES, LANES))],
        out_specs=batch(GROUP_W),
        out_shape=jax.ShapeDtypeStruct((B, T, GROUP_W), BF16),
        scratch_shapes=[pltpu.VMEM((T + 2 * NSA_CMP_STRIDE, LANES), F32),
                        pltpu.VMEM((NC, NSA_CMP_LEN * LANES), BF16),
                        pltpu.VMEM((GROUP_HEADS, T, LANES), BF16),
                        pltpu.VMEM((NC, LANES), BF16), pltpu.VMEM((NC, LANES), BF16)]
        + [pltpu.VMEM((T, LANES), BF16) for _ in range(4)]
        + [pltpu.VMEM((2, GROUP_HEADS * TQ, LANES), F32) for _ in range(2)]
        + [pltpu.VMEM((T // NSA_SEL_BLOCK, TQ), F32)],
        compiler_params=pltpu.CompilerParams(dimension_semantics=("arbitrary",),
                                             vmem_limit_bytes=VMEM_LIMIT),
        name="nsa",
    )(g_ns, g_sm, cosq, sinq, cosk, sink, cosc, sinc, pe, w1, w2)


def _out_kernel(y_hg, y_fx, y_sb, y_ns, gate_ref, x_ref, gn_ref, w_ref, fg_ref, o_ref, *, final):
    bd = jnp.where(_div(_iota((2 * LANES, LANES), 0) & (LANES - 1), HEAD_DIM)
                   == _div(_iota((2 * LANES, LANES), 1), HEAD_DIM), 1.0, 0.0).astype(BF16)
    parts = []
    for ref in (y_hg, y_fx, y_sb, y_ns):
        y = ref[...].astype(F32)
        for j in range(GROUP_W // LANES):
            yy = y[:, j * LANES:(j + 1) * LANES]
            hi, lo = _split2(yy * yy)
            ms = _nn(jnp.concatenate([hi, lo], axis=1), bd) * (1.0 / HEAD_DIM)
            parts.append(yy * lax.rsqrt(ms + EPS))
    gt = gate_ref[...].astype(F32)
    z = jnp.concatenate(parts, axis=1) * gn_ref[...] * (gt * _sigmoid(gt))
    out = x_ref[...] + _nn(z.astype(BF16), w_ref[...])
    if final:
        out = out * lax.rsqrt(jnp.mean(out * out, axis=-1, keepdims=True) + EPS) * fg_ref[...]
    o_ref[...] = out


def _out_project(ys, gate, x2, gn, w_out, fg, final, tm):
    n = x2.shape[0]
    kern = functools.partial(_out_kernel, final=final)
    row = lambda wd: pl.BlockSpec((tm, wd), lambda i: (i, 0))
    return pl.pallas_call(
        kern,
        grid=(n // tm,),
        in_specs=[row(GROUP_W)] * 4 + [row(D_MODEL), row(D_MODEL),
                                       pl.BlockSpec((1, D_MODEL), lambda i: (0, 0)),
                                       pl.BlockSpec((D_MODEL, D_MODEL), lambda i: (0, 0)),
                                       pl.BlockSpec((1, D_MODEL), lambda i: (0, 0))],
        out_specs=row(D_MODEL),
        out_shape=jax.ShapeDtypeStruct((n, D_MODEL), F32),
        compiler_params=pltpu.CompilerParams(dimension_semantics=("arbitrary",),
                                             vmem_limit_bytes=VMEM_LIMIT),
        name="outproj",
    )(*ys, gate, x2, gn.reshape(1, D_MODEL), w_out, fg.reshape(1, D_MODEL))


def _pack_in_weights(w):
    o = IN_OFFS
    sm = jnp.concatenate([w[:, o[6]:o[7]], w[:, o[17]:o[18]],
                          jnp.zeros((D_MODEL, LANES - GROUP_HEADS - 3 * GROUP_HEADS), w.dtype)], axis=1)
    groups = [w[:, o[0]:o[3]], w[:, o[3]:o[6]], w[:, o[7]:o[10]], w[:, o[10]:o[17]], w[:, o[18]:o[19]], sm]
    return [g.astype(BF16) for g in groups]


def _pack_cmp_weights(pe_k, w1_k, w2_k, pe_v, w1_v, w2_v):
    L, d = NSA_CMP_LEN, HEAD_DIM
    z = jnp.zeros((L, d, d), F32)
    w1 = jnp.concatenate([jnp.concatenate([w1_k.reshape(L, d, d), z], axis=2),
                          jnp.concatenate([z, w1_v.reshape(L, d, d)], axis=2)], axis=1).reshape(L * 2 * d, 2 * d)
    z2 = jnp.zeros((d, d), F32)
    w2 = jnp.concatenate([jnp.concatenate([w2_k, z2], axis=1), jnp.concatenate([z2, w2_v], axis=1)], axis=0)
    return jnp.concatenate([pe_k, pe_v], axis=1), w1.astype(BF16), w2.astype(BF16)


def kernel(x, norm_g, w_in, hgrn_lb_logits, fox_fb, nsa_cmp_pe_k, nsa_cmp_w1_k, nsa_cmp_w2_k,
           nsa_cmp_pe_v, nsa_cmp_w1_v, nsa_cmp_w2_v, out_norm_g, w_out, final_norm_g):
    B, T, D = x.shape
    depth = w_in.shape[0]
    x2 = x.reshape(B * T, D)
    for l in range(depth):
        g_hg, g_fx, g_sb, g_ns, g_gate, g_sm = _project(x2, norm_g[l], _pack_in_weights(w_in[l]), 512)
        g_sm3 = g_sm.reshape(B, T, LANES)
        fb = jnp.concatenate([fox_fb[l], jnp.zeros((LANES - GROUP_HEADS,), F32)]).reshape(1, LANES)
        pe, w1, w2 = _pack_cmp_weights(nsa_cmp_pe_k[l], nsa_cmp_w1_k[l], nsa_cmp_w2_k[l],
                                       nsa_cmp_pe_v[l], nsa_cmp_w1_v[l], nsa_cmp_w2_v[l])
        y_hg = _hgrn2(g_hg.reshape(B, T, -1), hgrn_lb_logits, l, B, T)
        y_fx, y_sb = _fox_stick_breaking(g_fx.reshape(B, T, -1), g_sm3, fb, g_sb.reshape(B, T, -1), B, T)
        y_ns = _nsa(g_ns.reshape(B, T, -1), g_sm3, pe, w1, w2, B, T)
        ys = [y.reshape(B * T, GROUP_W) for y in (y_hg, y_fx, y_sb, y_ns)]
        x2 = _out_project(ys, g_gate, x2, out_norm_g[l], w_out[l].astype(BF16), final_norm_g,
                          l == depth - 1, 512)
    return x2.reshape(B, T, D)
```

```python
import functools

import numpy as np
import jax
import jax.numpy as jnp
from jax import lax
from jax.experimental import pallas as pl
from jax.experimental.pallas import tpu as pltpu

F32 = jnp.float32
BF16 = jnp.bfloat16

D_MODEL = 1024
HEAD_DIM = 64
GROUP_W = 256
GROUP_HEADS = 4
ATTN_SCALE = HEAD_DIM ** -0.5
LOG2E = 1.4426950408889634
SCALE_LOG2 = ATTN_SCALE * LOG2E
ROPE_THETA = 500000.0
ROT_DIM = HEAD_DIM // 4
HG_CHUNK = 64
HG_VPU_MIN_HALF = 8
NSA_CMP_LEN = 32
NSA_CMP_STRIDE = 16
NSA_SEL_BLOCK = 64
NSA_TOP_N = 16
NSA_WINDOW = 512
NSA_FORCE = 1.0e4
NEG_BIG = -1.0e30
SEL_OFF = -(2.0 ** 100)
EPS = 1e-6

LANES = 128
VMEM_LIMIT = 56 * 1024 * 1024

IN_WIDTHS = (GROUP_W,) * 3 + (GROUP_W,) * 3 + (GROUP_HEADS,) + (GROUP_W,) * 3 + (GROUP_W,) \
    + (HEAD_DIM,) * 6 + (3 * GROUP_HEADS,) + (D_MODEL,)
IN_OFFS = np.concatenate([[0], np.cumsum(IN_WIDTHS)]).tolist()
GATE_GROUP = 4


def _nn(a, b):
    return jnp.dot(a, b, preferred_element_type=F32)


def _nt(a, b):
    return lax.dot_general(a, b, (((1,), (1,)), ((), ())), preferred_element_type=F32)


def _tn(a, b):
    return lax.dot_general(a, b, (((0,), (0,)), ((), ())), preferred_element_type=F32)


def _iota(shape, dim):
    return lax.broadcasted_iota(jnp.int32, shape, dim)


def _split2(x):
    hi = x.astype(BF16)
    lo = (x - hi.astype(F32)).astype(BF16)
    return hi, lo


def _split3(x):
    p1 = x.astype(BF16).astype(F32)
    r = x - p1
    p2 = r.astype(BF16).astype(F32)
    p3 = (r - p2).astype(BF16).astype(F32)
    return p1, p2, p3


def _sigmoid(x):
    return 1.0 / (1.0 + jnp.exp(-x))


def _log_sigmoid(x):
    return jnp.minimum(x, 0.0) - jnp.log(1.0 + jnp.exp(-jnp.abs(x)))


def _div(x, n):
    assert n & (n - 1) == 0
    return x >> (n.bit_length() - 1)


def _rows(i, n):
    return pl.ds(pl.multiple_of(i * n, n), n)


def _head_to_slot(x, h):
    return x if h % 2 == 0 else pltpu.roll(x, HEAD_DIM, 1)


def _proj_kernel(x_ref, g_ref, w_hg, w_fx, w_sb, w_ns, w_gate, w_sm,
                 o_hg, o_fx, o_sb, o_ns, o_gate, o_sm):
    x = x_ref[...]
    h = x * lax.rsqrt(jnp.mean(x * x, axis=-1, keepdims=True) + EPS) * g_ref[...]
    hb = h.astype(BF16)
    for w, o in ((w_hg, o_hg), (w_fx, o_fx), (w_sb, o_sb), (w_ns, o_ns), (w_gate, o_gate), (w_sm, o_sm)):
        o[...] = _nn(hb, w[...]).astype(o.dtype)


def _project(x2, norm_g, ws, tm):
    n = x2.shape[0]
    widths = [w.shape[1] for w in ws]
    return pl.pallas_call(
        _proj_kernel,
        grid=(n // tm,),
        in_specs=[pl.BlockSpec((tm, D_MODEL), lambda i: (i, 0)),
                  pl.BlockSpec((1, D_MODEL), lambda i: (0, 0))]
        + [pl.BlockSpec((D_MODEL, wd), lambda i: (0, 0)) for wd in widths],
        out_specs=[pl.BlockSpec((tm, wd), lambda i: (i, 0)) for wd in widths],
        out_shape=[jax.ShapeDtypeStruct((n, wd), BF16 if gi == GATE_GROUP else F32) for gi, wd in enumerate(widths)],
        compiler_params=pltpu.CompilerParams(dimension_semantics=("arbitrary",),
                                             vmem_limit_bytes=VMEM_LIMIT),
        name="proj",
    )(x2, norm_g.reshape(1, D_MODEL), *ws)


def _fox_program(g_ref, sm_ref, fb_ref, o_ref, cum_ref, qa_ref, ka_ref, va_ref, m_ref, acc_ref, *, T, TQ, TK):
    H = GROUP_HEADS
    RB = 256
    NB = T // LANES
    tri = jnp.where(_iota((LANES, LANES), 0) >= _iota((LANES, LANES), 1), 1.0, 0.0).astype(BF16)
    fb = fb_ref[...]

    local = []
    for i in range(NB):
        p1, p2, p3 = _split3(_log_sigmoid(sm_ref[i * LANES:(i + 1) * LANES, :] + fb))
        local.append(_nn(tri, p1.astype(BF16)) + _nn(tri, p2.astype(BF16)) + _nn(tri, p3.astype(BF16)))
    carry = jnp.zeros((1, LANES), F32)
    for i in range(NB):
        c = local[i] + carry
        cum_ref[i * LANES:(i + 1) * LANES, :] = c
        carry = c[LANES - 1:LANES, :]

    lane = _iota((RB, LANES), 1)
    upper = lane >= HEAD_DIM

    def build_body(i, _):
        r = _rows(i, RB)
        cum = cum_ref[r, :]
        for h in range(H):
            j = h // 2
            own = upper if h % 2 else ~upper
            a0 = 0 if h % 2 else HEAD_DIM
            q = g_ref[r, j * LANES:(j + 1) * LANES] * SCALE_LOG2
            k = g_ref[r, GROUP_W + j * LANES:GROUP_W + (j + 1) * LANES]
            v = g_ref[r, 2 * GROUP_W + j * LANES:2 * GROUP_W + (j + 1) * LANES]
            qa = jnp.where(own, q, 0.0)
            ka = jnp.where(own, k, 0.0)
            for p, c in enumerate(_split3(jnp.broadcast_to(cum[:, h:h + 1], (RB, LANES)) * LOG2E)):
                qa = jnp.where(lane == a0 + p, c, qa)
                qa = jnp.where(lane == a0 + 3 + p, 1.0, qa)
                ka = jnp.where(lane == a0 + p, 1.0, ka)
                ka = jnp.where(lane == a0 + 3 + p, -c, ka)
            qa_ref[h, r, :] = qa.astype(BF16)
            ka_ref[h, r, :] = ka.astype(BF16)
            va_ref[h, r, :] = jnp.where(own, v, 1.0).astype(BF16)
        return 0

    lax.fori_loop(0, T // RB, build_body, 0)

    NSUB = TQ // TK
    upper_q =_iota((TQ, LANES), 1) >= HEAD_DIM

    def start(i):
        for h in range(H):
            m_ref[h] = jnp.full((TQ, LANES), NEG_BIG, F32)
            acc_ref[h] = jnp.zeros((TQ, LANES), F32)
        return [qa_ref[h, _rows(i, TQ), :] for h in range(H)]

    def k_step(qas, i, j, masked, r0=0):
        rk = _rows(j, TK)
        rs = slice(r0, TQ)
        for h in range(H):
            s = _nt(qas[h][rs], ka_ref[h, rk, :])
            if masked:
                off = _iota((TQ - r0, TK), 1) - _iota((TQ - r0, TK), 0)
                s = jnp.where(off <= i * TQ + r0 - j * TK, s, NEG_BIG)
            m = m_ref[h, rs, :]
            m_new = jnp.maximum(m, jnp.max(s, axis=-1, keepdims=True))
            p = jnp.exp2(jnp.concatenate([s[:, c0:c0 + LANES] - m_new for c0 in range(0, TK, LANES)], axis=1))
            acc_ref[h, rs, :] = jnp.exp2(m - m_new) * acc_ref[h, rs, :] + _nn(p.astype(BF16), va_ref[h, rk, :])
            m_ref[h, rs, :] = m_new

    def diag(qas, i):
        for d in range(NSUB):
            k_step(qas, i, i * NSUB + d, True, r0=d * TK)

    def pair(qas, i, jj):
        for d in range(NSUB):
            k_step(qas, i, jj * NSUB + d, False)

    def finish(i):
        for j in range(H // 2):
            ae, ao = acc_ref[2 * j], acc_ref[2 * j + 1]
            den = pltpu.roll(jnp.where(upper_q, ae, ao), HEAD_DIM, 1)
            o_ref[_rows(i, TQ), j * LANES:(j + 1) * LANES] = (jnp.where(upper_q, ao, ae) / den).astype(BF16)

    return start, diag, pair, finish


def _sb_program(g_ref, o_ref, qa_ref, ka_ref, va_ref, c_ref, acc_ref, *, T, TQ, TK):
    H = GROUP_HEADS
    RB = 256
    SUB = LANES
    NSUB = TQ // TK
    upper = _iota((RB, LANES), 1) >= HEAD_DIM

    def build_body(i, _):
        r = _rows(i, RB)
        for h in range(H):
            j = h // 2
            own = upper if h % 2 else ~upper
            q = g_ref[r, j * LANES:(j + 1) * LANES] * ATTN_SCALE
            k = g_ref[r, GROUP_W + j * LANES:GROUP_W + (j + 1) * LANES]
            v = g_ref[r, 2 * GROUP_W + j * LANES:2 * GROUP_W + (j + 1) * LANES]
            qa_ref[h, r, :] = jnp.where(own, q, 0.0).astype(BF16)
            ka_ref[h, r, :] = jnp.where(own, k, 0.0).astype(BF16)
            va_ref[h, r, :] = jnp.where(own, v, 0.0).astype(BF16)
        return 0

    lax.fori_loop(0, T // RB, build_body, 0)

    wr = _iota((2 * SUB, 2 * SUB), 0) & (SUB - 1)
    wc = _iota((2 * SUB, 2 * SUB), 1)
    suf_w = jnp.where((wc >= SUB) | (wr > wc), 1.0, 0.0).astype(BF16)

    def start(i):
        for h in range(H):
            c_ref[h] = jnp.zeros((TQ, LANES), F32)
        for j in range(H // 2):
            acc_ref[j] = jnp.zeros((TQ, LANES), F32)
        return [qa_ref[h, _rows(i, TQ), :] for h in range(H)]

    def k_step(qas, i, j, masked, r0=0):
        rk = _rows(j, TK)
        rs = slice(r0, TQ)
        for h in range(H):
            z = _nt(qas[h][rs], ka_ref[h, rk, :])
            ls = jnp.minimum(z, 0.0) - jnp.log(1.0 + jnp.exp2(jnp.abs(z) * -LOG2E))
            lom = ls - z
            if masked:
                off = _iota((TQ - r0, TK), 1) - _iota((TQ - r0, TK), 0)
                msk = off < i * TQ + r0 - j * TK
                lom = jnp.where(msk, lom, 0.0)
            c = c_ref[h, rs, :]
            parts = []
            for sb in reversed(range(TK // SUB)):
                hi, lo = _split2(lom[:, sb * SUB:(sb + 1) * SUB])
                cs = _nn(jnp.concatenate([hi, lo], axis=1), suf_w)
                parts.append(ls[:, sb * SUB:(sb + 1) * SUB] + cs[:, :SUB] + c)
                c = c + cs[:, SUB:]
            c_ref[h, rs, :] = c
            a = jnp.exp(jnp.concatenate(parts[::-1], axis=1))
            if masked:
                a = jnp.where(msk, a, 0.0)
            acc_ref[h // 2, rs, :] += _nn(a.astype(BF16), va_ref[h, rk, :])

    def diag(qas, i):
        for d in reversed(range(NSUB)):
            k_step(qas, i, i * NSUB + d, True, r0=d * TK)

    def pair(qas, i, jj):
        for d in range(NSUB):
            k_step(qas, i, (i - jj) * NSUB - 1 - d, False)

    def finish(i):
        for j in range(H // 2):
            o_ref[_rows(i, TQ), j * LANES:(j + 1) * LANES] = acc_ref[j].astype(BF16)

    return start, diag, pair, finish


def _fox_sb_kernel(fx_ref, sm_ref, fb_ref, sb_ref, ofx_ref, osb_ref,
                   cum_ref, fqa, fka, fva, fm, facc, sqa, ska, sva, sc, sacc, *, T, TQ, TK):
    fox = _fox_program(fx_ref, sm_ref, fb_ref, ofx_ref, cum_ref, fqa, fka, fva, fm, facc, T=T, TQ=TQ, TK=TK)
    stb = _sb_program(sb_ref, osb_ref, sqa, ska, sva, sc, sacc, T=T, TQ=TQ, TK=TK)
    progs = (fox, stb)

    def q_body(i, _):
        ctx = [p[0](i) for p in progs]
        for p, qas in zip(progs, ctx):
            p[1](qas, i)

        def k_body(jj, _):
            for p, qas in zip(progs, ctx):
                p[2](qas, i, jj)
            return 0

        lax.fori_loop(0, i, k_body, 0)
        for p in progs:
            p[3](i)
        return 0

    lax.fori_loop(0, T // TQ, q_body, 0)


def _fox_stick_breaking(g_fx, g_sm, fb, g_sb, B, T):
    TQ, TK = 512, 256
    kern = functools.partial(_fox_sb_kernel, T=T, TQ=TQ, TK=TK)
    slot = lambda: pltpu.VMEM((GROUP_HEADS, T, LANES), BF16)
    state = lambda n: pltpu.VMEM((n, TQ, LANES), F32)
    batch = lambda wd: pl.BlockSpec((None, T, wd), lambda b: (b, 0, 0))
    return pl.pallas_call(
        kern,
        grid=(B,),
        in_specs=[batch(3 * GROUP_W), batch(LANES), pl.BlockSpec((1, LANES), lambda b: (0, 0)), batch(3 * GROUP_W)],
        out_specs=[batch(GROUP_W), batch(GROUP_W)],
        out_shape=[jax.ShapeDtypeStruct((B, T, GROUP_W), BF16)] * 2,
        scratch_shapes=[pltpu.VMEM((T, LANES), F32), slot(), slot(), slot(), state(GROUP_HEADS), state(GROUP_HEADS),
                        slot(), slot(), slot(), state(GROUP_HEADS), state(GROUP_HEADS // 2)],
        compiler_params=pltpu.CompilerParams(dimension_semantics=("arbitrary",),
                                             vmem_limit_bytes=VMEM_LIMIT),
        name="fox_stickbreak",
    )(g_fx, g_sm, fb, g_sb)


def _hgrn_decay_matrix(C):
    nl = C.bit_length() - 1
    t = _iota((C, C), 0)
    u = _iota((C, C), 1)
    blocks = [u <= t]
    q_blocks, k_blocks = [], []
    for l in range(nl):
        m = C >> (l + 1)
        if m >= HG_VPU_MIN_HALF:
            continue
        bnd = (t & ~(2 * m - 1)) + (m - 1)
        q_blocks.append((u > bnd) & (u <= t))
        k_blocks.append((u > t) & (u <= bnd))
    d = jnp.concatenate([jnp.where(b, 1.0, 0.0) for b in blocks + q_blocks + k_blocks], axis=0).astype(BF16)
    return jnp.concatenate([d, d, d], axis=1)


def _hgrn_kernel(g_ref, lb_ref, o_ref, st_ref, ex_ref, *, T, C, NCH, layer):
    H = GROUP_HEADS
    nl = C.bit_length() - 1
    nfine = HG_VPU_MIN_HALF.bit_length() - 1
    ncoarse = nl - nfine
    lg = lb_ref[...]
    e = jnp.exp(lg - jnp.max(lg, axis=0, keepdims=True))
    sm = e / jnp.sum(e, axis=0, keepdims=True)
    lb = jnp.sum(sm[0:layer + 1, :], axis=0, keepdims=True) - sm[0:1, :]

    dmat = _hgrn_decay_matrix(C)
    lane_head4 = _div(_iota((H * C, GROUP_W), 1), HEAD_DIM)
    row_head4 = _div(_iota((H * C, GROUP_W), 0), C)
    head_sel = lane_head4 == row_head4
    tq_c = _iota((C, GROUP_W), 0)
    t4 = _iota((C, H * C), 0)
    s4 = _iota((C, H * C), 1) & (C - 1)
    bd_mask =_div(_iota((GROUP_W, GROUP_W), 0), HEAD_DIM) == _div(_iota((GROUP_W, GROUP_W), 1), HEAD_DIM)
    st_ref[...] = jnp.zeros((GROUP_W, GROUP_W), F32)

    def decays(r, slot, n):
        g = jnp.log(lb + (1.0 - lb) * _sigmoid(g_ref[r, GROUP_W:2 * GROUP_W]))
        g1, g2, g3 = _split3(g)
        ex_ref[slot, n] = _nn(dmat, jnp.concatenate([g1.astype(BF16), g2.astype(BF16), g3.astype(BF16)], axis=0))

    def intra(r, slot, n):
        ex = ex_ref.at[slot, n]
        qraw = g_ref[r, 0:GROUP_W]
        f = g_ref[r, GROUP_W:2 * GROUP_W]
        v = g_ref[r, 2 * GROUP_W:3 * GROUP_W].astype(BF16)
        q = qraw * _sigmoid(qraw)
        k = (1.0 - lb) * (1.0 - _sigmoid(f))
        b = ex[0:C]

        a = jnp.zeros((C, H * C), F32)
        for l in range(nl + 1):
            if l < nl:
                m = C >> (l + 1)
                if m >= HG_VPU_MIN_HALF:
                    bnd = jnp.concatenate([jnp.broadcast_to(b[s0 + m - 1:s0 + m, :], (2 * m, GROUP_W))
                                           for s0 in range(0, C, 2 * m)], axis=0)
                    dq, dk = b - bnd, bnd - b
                else:
                    lf = l - ncoarse
                    dq = ex[(1 + lf) * C:(2 + lf) * C]
                    dk = ex[(1 + nfine + lf) * C:(2 + nfine + lf) * C]
                qs = q * jnp.where((tq_c & m) != 0, jnp.exp(dq), 0.0)
                ks = k * jnp.where((tq_c & m) == 0, jnp.exp(dk), 0.0)
                pair = (t4 & ~(2 * m - 1)) == (s4 & ~(2 * m - 1))
            else:
                qs, ks = q, k
                pair = t4 == s4
            ksb = ks.astype(BF16)
            kst = jnp.where(head_sel, jnp.concatenate([ksb] * H, axis=0), jnp.zeros((), BF16))
            a = a + jnp.where(pair, _nt(qs.astype(BF16), kst), 0.0)
        vbd = jnp.where(head_sel, jnp.concatenate([v] * H, axis=0), jnp.zeros((), BF16))
        o_intra = _nn(a.astype(BF16), vbd)
        qb =(q * jnp.exp(b)).astype(BF16)
        kd = (k * jnp.exp(b[C - 1:C, :] - b)).astype(BF16)
        upd = jnp.where(bd_mask, _tn(v, kd), 0.0)
        return o_intra, qb, upd, jnp.exp(b[C - 1:C, :])

    NIT = T // (C * NCH)
    for n in range(NCH):
        decays(_rows(n, C), 0, n)

    def body(ci, _):
        slot = ci & 1
        nxt = jnp.minimum(ci + 1, NIT - 1)
        for n in range(NCH):
            decays(_rows(nxt * NCH + n, C), 1 - slot, n)
        rs = [_rows(ci * NCH + n, C) for n in range(NCH)]
        parts = [intra(r, slot, n) for n, r in enumerate(rs)]
        st = st_ref[...]
        for r, (o_intra, qb, upd, decay) in zip(rs, parts):
            o_ref[r, :] = (o_intra + _nt(qb, st.astype(BF16))).astype(BF16)
            st = st * decay + upd
        st_ref[...] = st
        return 0

    lax.fori_loop(0, NIT, body, 0)


def _hgrn2(g_hg, lb_logits, layer, B, T):
    depth = lb_logits.shape[0]
    NCH = 8
    kern = functools.partial(_hgrn_kernel, T=T, C=HG_CHUNK, NCH=NCH, layer=layer)
    return pl.pallas_call(
        kern,
        grid=(B,),
        in_specs=[pl.BlockSpec((None, T, 3 * GROUP_W), lambda b: (b, 0, 0)),
                  pl.BlockSpec((depth, GROUP_W), lambda b: (0, 0))],
        out_specs=pl.BlockSpec((None, T, GROUP_W), lambda b: (b, 0, 0)),
        out_shape=jax.ShapeDtypeStruct((B, T, GROUP_W), BF16),
        scratch_shapes=[pltpu.VMEM((GROUP_W, GROUP_W), F32),
                        pltpu.VMEM((2, NCH, (2 * HG_VPU_MIN_HALF.bit_length() - 1) * HG_CHUNK, GROUP_W), F32)],
        compiler_params=pltpu.CompilerParams(dimension_semantics=("arbitrary",),
                                             vmem_limit_bytes=VMEM_LIMIT),
        name="hgrn2",
    )(g_hg, lb_logits)


def _rope(x, cos, sin, lane):
    swapped = jnp.where((lane & (HEAD_DIM - 1)) < ROT_DIM // 2,
                        pltpu.roll(x, LANES - ROT_DIM // 2, 1), pltpu.roll(x, ROT_DIM // 2, 1))
    return x * cos + swapped * sin


def _nsa_kernel(g_ref, sm_ref, cosq_ref, sinq_ref, cosk_ref, sink_ref, cosc_ref, sinc_ref,
                pe_ref, w1_ref, w2_ref, o_ref,
                cv_ref, ablk_ref, qa_ref, kcmp_ref, vcmp_ref, ksel_ref, vsel_ref, kwin_ref, vwin_ref,
                m_ref, acc_ref, score_ref, *, T, TQ, TK):
    H = GROUP_HEADS
    RB = 256
    NC = T // NSA_CMP_STRIDE
    NSEL = T // NSA_SEL_BLOCK
    TOPN = min(NSA_TOP_N, NSEL)
    WT = NSA_WINDOW // TK
    SEL_U = 8
    lane = _iota((RB, LANES), 1)
    grow = _iota((RB, LANES), 0)

    def build_body(i, _):
        r = _rows(i, RB)
        cosq, sinq = cosq_ref[r, :], sinq_ref[r, :]
        cosk, sink = cosk_ref[r, :], sink_ref[r, :]
        for j in range(2):
            xr = _rope(g_ref[r, j * LANES:(j + 1) * LANES], cosq, sinq, lane) * SCALE_LOG2
            for h in (2 * j, 2 * j + 1):
                qa_ref[h, r, :] = jnp.where(lane < HEAD_DIM, _head_to_slot(xr, h), 0.0).astype(BF16)
        cv_ref[r, :] = g_ref[r, GROUP_W:GROUP_W + LANES]
        ksvs = g_ref[r, GROUP_W + LANES:GROUP_W + 2 * LANES]
        kwvw = g_ref[r, GROUP_W + 2 * LANES:GROUP_W + 3 * LANES]
        blk = _div(i * RB + grow, NSA_SEL_BLOCK)
        onehot = jnp.where((lane >= HEAD_DIM) & (lane - HEAD_DIM == blk), 1.0, 0.0)
        ksel_ref[r, :] = jnp.where(lane < HEAD_DIM, _rope(ksvs, cosk, sink, lane), onehot).astype(BF16)
        vsel_ref[r, :] = jnp.where(lane < HEAD_DIM, pltpu.roll(ksvs, HEAD_DIM, 1), 1.0).astype(BF16)
        kwin_ref[r, :] = jnp.where(lane < HEAD_DIM, _rope(kwvw, cosk, sink, lane), 0.0).astype(BF16)
        vwin_ref[r, :] = jnp.where(lane < HEAD_DIM, pltpu.roll(kwvw, HEAD_DIM, 1), 1.0).astype(BF16)
        return 0

    lax.fori_loop(0, T // RB, build_body, 0)
    cv_ref[T:T + 2 * NSA_CMP_STRIDE, :] = jnp.zeros((2 * NSA_CMP_STRIDE, LANES), F32)

    for l in range(NSA_CMP_LEN):
        blk_l = cv_ref[pl.ds(l, NC, stride=NSA_CMP_STRIDE), :] + pe_ref[l:l + 1, :]
        ablk_ref[:, l * LANES:(l + 1) * LANES] = blk_l.astype(BF16)
    hid = _nn(ablk_ref[...], w1_ref[...])
    hid = hid * _sigmoid(hid)
    kv = _nn(hid.astype(BF16), w2_ref[...])
    lane_c = _iota((NC, LANES), 1)
    kcmp_ref[...] = jnp.where(lane_c < HEAD_DIM, _rope(kv, cosc_ref[...], sinc_ref[...], lane_c), 0.0).astype(BF16)
    vcmp_ref[...] = jnp.where(lane_c < HEAD_DIM, pltpu.roll(kv, HEAD_DIM, 1), 0.0).astype(BF16)

    on = _iota((NSEL, 2 * NC), 1) & (NC - 1)
    oj = _iota((NSEL, 2 * NC), 0)
    ovt = jnp.clip(jnp.minimum(on * NSA_CMP_STRIDE + NSA_CMP_LEN, oj * NSA_SEL_BLOCK + NSA_SEL_BLOCK)
                   - jnp.maximum(on * NSA_CMP_STRIDE, oj * NSA_SEL_BLOCK), 0, None).astype(F32) / NSA_CMP_LEN
    ovt = jnp.where(on < NC - 1, ovt, 0.0).astype(BF16)
    jrow = _iota((NSEL, TQ), 0)
    tcol = _iota((NSEL, TQ), 1)

    gr = _iota((2 * LANES, 3 * GROUP_W), 0) & (LANES - 1)
    gc = _iota((2 * LANES, 3 * GROUP_W), 1)
    gate_w = jnp.where(gr == GROUP_HEADS + _div(gc, GROUP_W) * GROUP_HEADS + _div(gc & (GROUP_W - 1), HEAD_DIM),
                       1.0, 0.0).astype(BF16)

    row4 = _iota((H * TQ, TK), 0) & (TQ - 1)
    col4 = _iota((H * TQ, TK), 1)
    lane4 = _iota((H * TQ, LANES), 1)
    lane_q = _iota((TQ, LANES), 1)
    row_q = _iota((TQ, LANES), 0)
    cmp_end = _iota((H * TQ, NC), 1) * NSA_CMP_STRIDE + (NSA_CMP_LEN - 1)
    rowc = _iota((H * TQ, NC), 0) & (TQ - 1)

    def q_body(i, _):
        rq = _rows(i, TQ)
        t0 = i * TQ
        qst = jnp.concatenate([qa_ref[h, rq, :] for h in range(H)], axis=0)

        valid = cmp_end <= (t0 + rowc)
        s = jnp.where(valid, _nt(qst, kcmp_ref[...]), NEG_BIG)
        p = jnp.where(valid, jnp.exp2(s - jnp.max(s, axis=-1, keepdims=True)), 0.0)
        den = jnp.sum(p, axis=-1, keepdims=True)
        p = p / jnp.where(den > 0, den, 1.0)
        o_cmp = _nn(p.astype(BF16), vcmp_ref[...])
        psum = p[0:TQ]
        for h in range(1, H):
            psum = psum + p[h * TQ:(h + 1) * TQ]
        hi, lo = _split2(psum)
        imp = _nt(ovt, jnp.concatenate([hi, lo], axis=1))

        qblk = _div(t0 + tcol, NSA_SEL_BLOCK)
        forced = (jrow == 0) | (jrow == qblk) | (jrow == qblk - 1)
        score = jnp.where(forced, NSA_FORCE, imp)
        score_ref[...] = jnp.where(jrow <= qblk, score, -NSA_FORCE)
        score = score_ref[...]
        rank = jnp.zeros((NSEL, TQ), F32)
        for ii in range(NSEL):
            ci = score_ref[ii:ii + 1, :]
            rank = rank + jnp.where((ci > score) | ((ci == score) & (ii < jrow)), 1.0, 0.0)
        sel = (rank < TOPN) & (jrow <= qblk)
        bias = jnp.where(sel, 0.0, SEL_OFF)
        aug = jnp.concatenate([jnp.zeros((HEAD_DIM, TQ), F32), bias,
                               jnp.zeros((LANES - HEAD_DIM - NSEL, TQ), F32)], axis=0).T.astype(BF16)
        qsel = jnp.where(lane4 < HEAD_DIM, qst, jnp.concatenate([aug] * H, axis=0))

        def att_step(j, _, slot, q_in, k_ref, v_ref, mode):
            rk = _rows(j, TK)
            s = _nt(q_in, k_ref[rk, :])
            if mode is not None:
                kpos = j * TK + col4
                qpos = t0 + row4
                msk = (kpos <= qpos) if mode == "causal" else (kpos > qpos - NSA_WINDOW)
                s = jnp.where(msk, s, NEG_BIG)
            m = m_ref[slot]
            m_new = jnp.maximum(m, jnp.max(s, axis=-1, keepdims=True))
            p = jnp.exp2(jnp.concatenate([s[:, c0:c0 + LANES] - m_new for c0 in range(0, TK, LANES)], axis=1))
            if mode is not None:
                p = jnp.where(msk, p, 0.0)
            acc_ref[slot] = jnp.exp2(m - m_new) * acc_ref[slot] + _nn(p.astype(BF16), v_ref[rk, :])
            m_ref[slot] = m_new
            return 0

        for slot in range(2):
            m_ref[slot] = jnp.full((H * TQ, LANES), NEG_BIG, F32)
            acc_ref[slot] = jnp.zeros((H * TQ, LANES), F32)
        sel_step = functools.partial(att_step, slot=0, q_in=qsel, k_ref=ksel_ref, v_ref=vsel_ref)
        def sel_group(jj, _):
            for u in range(SEL_U):
                sel_step(SEL_U * jj + u, 0, mode=None)
            return 0

        lax.fori_loop(0, _div(i, SEL_U), sel_group, 0)
        for rem in range(SEL_U):
            @pl.when((i & (SEL_U - 1)) == rem)
            def _(rem=rem):
                for u in range(rem, 0, -1):
                    sel_step(i - u, 0, mode=None)
                sel_step(i, 0, mode="causal")

        win_step = functools.partial(att_step, slot=1, q_in=qst, k_ref=kwin_ref, v_ref=vwin_ref)
        for nw in range(WT, -1, -1):
            @pl.when(jnp.minimum(i, WT) == nw)
            def _(nw=nw):
                for w in range(nw, 0, -1):
                    win_step(i - w, 0, mode="window" if w == WT else None)
                win_step(i, 0, mode="causal")

        hi, lo = _split2(_sigmoid(sm_ref[rq, :]))
        gmap = _nn(jnp.concatenate([hi, lo], axis=1), gate_w)
        low = lane_q < HEAD_DIM
        for j in range(H // 2):
            re = slice(2 * j * TQ, (2 * j + 1) * TQ)
            ro = slice((2 * j + 1) * TQ, (2 * j + 2) * TQ)
            out = gmap[:, j * LANES:(j + 1) * LANES] * jnp.where(low, o_cmp[re], pltpu.roll(o_cmp[ro], HEAD_DIM, 1))
            for slot in range(2):
                ae, ao = acc_ref[slot, re, :], acc_ref[slot, ro, :]
                num = jnp.where(low, ae, pltpu.roll(ao, HEAD_DIM, 1))
                den = jnp.where(low, pltpu.roll(ae, HEAD_DIM, 1), ao)
                c0 = (slot + 1) * GROUP_W + j * LANES
                out = out + gmap[:, c0:c0 + LANES] * (num / den)
            o_ref[rq, j * LANES:(j + 1) * LANES] = out.astype(BF16)
        return 0

    lax.fori_loop(0, T // TQ, q_body, 0)


def _rope_tables(pos):
    half = ROT_DIM // 2
    inv_freq = ROPE_THETA ** (-(jnp.arange(half, dtype=F32) * 2.0 / ROT_DIM))
    ang = pos.astype(F32)[:, None] * inv_freq[None, :]
    cos, sin = jnp.cos(ang), jnp.sin(ang)
    n = pos.shape[0]
    c64 = jnp.concatenate([cos, cos, jnp.ones((n, HEAD_DIM - ROT_DIM), F32)], axis=1)
    s64 = jnp.concatenate([-sin, sin, jnp.zeros((n, HEAD_DIM - ROT_DIM), F32)], axis=1)
    return c64, s64


def _nsa(g_ns, g_sm, pe, w1, w2, B, T):
    TQ = TK = 256
    NC = T // NSA_CMP_STRIDE
    c64, s64 = _rope_tables(jnp.arange(T))
    cosq, sinq = jnp.tile(c64, (1, 2)), jnp.tile(s64, (1, 2))
    cosk = jnp.concatenate([c64, jnp.ones((T, HEAD_DIM), F32)], axis=1)
    sink = jnp.concatenate([s64, jnp.zeros((T, HEAD_DIM), F32)], axis=1)
    cc, sc = _rope_tables(jnp.arange(NC) * NSA_CMP_STRIDE + NSA_CMP_LEN - 1)
    cosc = jnp.concatenate([cc, jnp.ones((NC, HEAD_DIM), F32)], axis=1)
    sinc = jnp.concatenate([sc, jnp.zeros((NC, HEAD_DIM), F32)], axis=1)
    kern = functools.partial(_nsa_kernel, T=T, TQ=TQ, TK=TK)
    full = lambda shape: pl.BlockSpec(shape, lambda b: (0,) * len(shape))
    batch = lambda wd: pl.BlockSpec((None, T, wd), lambda b: (b, 0, 0))
    ns_w = GROUP_W + 3 * LANES
    return pl.pallas_call(
        kern,
        grid=(B,),
        in_specs=[batch(ns_w), batch(LANES),
                  full((T, LANES)), full((T, LANES)), full((T, LANES)), full((T, LANES)),
                  full((NC, LANES)), full((NC, LANES)),
                  full((NSA_CMP_LEN, LANES)), full((NSA_CMP_LEN * LANES, LANES)), full((LANES, LANES))],
        out_specs=batch(GROUP_W),
        out_shape=jax.ShapeDtypeStruct((B, T, GROUP_W), BF16),
        scratch_shapes=[pltpu.VMEM((T + 2 * NSA_CMP_STRIDE, LANES), F32),
                        pltpu.VMEM((NC, NSA_CMP_LEN * LANES), BF16),
                        pltpu.VMEM((GROUP_HEADS, T, LANES), BF16),
                        pltpu.VMEM((NC, LANES), BF16), pltpu.VMEM((NC, LANES), BF16)]
        + [pltpu.VMEM((T, LANES), BF16) for _ in range(4)]
        + [pltpu.VMEM((2, GROUP_HEADS * TQ, LANES), F32) for _ in range(2)]
        + [pltpu.VMEM((T // NSA_SEL_BLOCK, TQ), F32)],
        compiler_params=pltpu.CompilerParams(dimension_semantics=("arbitrary",),
                                             vmem_limit_bytes=VMEM_LIMIT),
        name="nsa",
    )(g_ns, g_sm, cosq, sinq, cosk, sink, cosc, sinc, pe, w1, w2)


def _out_kernel(y_hg, y_fx, y_sb, y_ns, gate_ref, x_ref, gn_ref, w_ref, fg_ref, o_ref, *, final):
    bd = jnp.where(_div(_iota((2 * LANES, LANES), 0) & (LANES - 1), HEAD_DIM)
                   == _div(_iota((2 * LANES, LANES), 1), HEAD_DIM), 1.0, 0.0).astype(BF16)
    parts = []
    for ref in (y_hg, y_fx, y_sb, y_ns):
        y = ref[...].astype(F32)
        for j in range(GROUP_W // LANES):
            yy = y[:, j * LANES:(j + 1) * LANES]
            hi, lo = _split2(yy * yy)
            ms = _nn(jnp.concatenate([hi, lo], axis=1), bd) * (1.0 / HEAD_DIM)
            parts.append(yy * lax.rsqrt(ms + EPS))
    gt = gate_ref[...].astype(F32)
    z = jnp.concatenate(parts, axis=1) * gn_ref[...] * (gt * _sigmoid(gt))
    out = x_ref[...] + _nn(z.astype(BF16), w_ref[...])
    if final:
        out = out * lax.rsqrt(jnp.mean(out * out, axis=-1, keepdims=True) + EPS) * fg_ref[...]
    o_ref[...] = out


def _out_project(ys, gate, x2, gn, w_out, fg, final, tm):
    n = x2.shape[0]
    kern = functools.partial(_out_kernel, final=final)
    row = lambda wd: pl.BlockSpec((tm, wd), lambda i: (i, 0))
    return pl.pallas_call(
        kern,
        grid=(n // tm,),
        in_specs=[row(GROUP_W)] * 4 + [row(D_MODEL), row(D_MODEL),
                                       pl.BlockSpec((1, D_MODEL), lambda i: (0, 0)),
                                       pl.BlockSpec((D_MODEL, D_MODEL), lambda i: (0, 0)),
                                       pl.BlockSpec((1, D_MODEL), lambda i: (0, 0))],
        out_specs=row(D_MODEL),
        out_shape=jax.ShapeDtypeStruct((n, D_MODEL), F32),
        compiler_params=pltpu.CompilerParams(dimension_semantics=("arbitrary",),
                                             vmem_limit_bytes=VMEM_LIMIT),
        name="outproj",
    )(*ys, gate, x2, gn.reshape(1, D_MODEL), w_out, fg.reshape(1, D_MODEL))


def _pack_in_weights(w):
    o = IN_OFFS
    sm = jnp.concatenate([w[:, o[6]:o[7]], w[:, o[17]:o[18]],
                          jnp.zeros((D_MODEL, LANES - GROUP_HEADS - 3 * GROUP_HEADS), w.dtype)], axis=1)
    groups = [w[:, o[0]:o[3]], w[:, o[3]:o[6]], w[:, o[7]:o[10]], w[:, o[10]:o[17]], w[:, o[18]:o[19]], sm]
    return [g.astype(BF16) for g in groups]


def _pack_cmp_weights(pe_k, w1_k, w2_k, pe_v, w1_v, w2_v):
    L, d = NSA_CMP_LEN, HEAD_DIM
    z = jnp.zeros((L, d, d), F32)
    w1 = jnp.concatenate([jnp.concatenate([w1_k.reshape(L, d, d), z], axis=2),
                          jnp.concatenate([z, w1_v.reshape(L, d, d)], axis=2)], axis=1).reshape(L * 2 * d, 2 * d)
    z2 = jnp.zeros((d, d), F32)
    w2 = jnp.concatenate([jnp.concatenate([w2_k, z2], axis=1), jnp.concatenate([z2, w2_v], axis=1)], axis=0)
    return jnp.concatenate([pe_k, pe_v], axis=1), w1.astype(BF16), w2.astype(BF16)


def kernel(x, norm_g, w_in, hgrn_lb_logits, fox_fb, nsa_cmp_pe_k, nsa_cmp_w1_k, nsa_cmp_w2_k,
           nsa_cmp_pe_v, nsa_cmp_w1_v, nsa_cmp_w2_v, out_norm_g, w_out, final_norm_g):
    B, T, D = x.shape
    depth = w_in.shape[0]
    x2 = x.reshape(B * T, D)
    for l in range(depth):
        g_hg, g_fx, g_sb, g_ns, g_gate, g_sm = _project(x2, norm_g[l], _pack_in_weights(w_in[l]), 512)
        g_sm3 = g_sm.reshape(B, T, LANES)
        fb = jnp.concatenate([fox_fb[l], jnp.zeros((LANES - GROUP_HEADS,), F32)]).reshape(1, LANES)
        pe, w1, w2 = _pack_cmp_weights(nsa_cmp_pe_k[l], nsa_cmp_w1_k[l], nsa_cmp_w2_k[l],
                                       nsa_cmp_pe_v[l], nsa_cmp_w1_v[l], nsa_cmp_w2_v[l])
        y_hg = _hgrn2(g_hg.reshape(B, T, -1), hgrn_lb_logits, l, B, T)
        y_fx, y_sb = _fox_stick_breaking(g_fx.reshape(B, T, -1), g_sm3, fb, g_sb.reshape(B, T, -1), B, T)
        y_ns = _nsa(g_ns.reshape(B, T, -1), g_sm3, pe, w1, w2, B, T)
        ys = [y.reshape(B * T, GROUP_W) for y in (y_hg, y_fx, y_sb, y_ns)]
        x2 = _out_project(ys, g_gate, x2, out_norm_g[l], w_out[l].astype(BF16), final_norm_g,
                          l == depth - 1, 512)
    return x2.reshape(B, T, D)
```
